```python
import jax
import jax.numpy as jnp
from jax import lax
import numpy as np

D_MODEL = 2048
BATCH = 4
SEQ = 4096
DEPTH = 4

GRID_W = 64
CTX_LEN = 256
N_EVEN = (DEPTH + 1) // 2
N_ODD = DEPTH // 2
EPS_RMS = 1e-6
D_A = D_MODEL // 2
HEAD_A = 64
H_A = D_A // HEAD_A
R_W = 64
R_A = 64
GN_EPS = 64e-5
D_B = D_MODEL // 2
HEAD_B = 64
H_B = D_B // HEAD_B
NA_WIN_H = 8
NA_WIN_W = 16
RG_BLOCKS = 16
D_C = (4 * D_MODEL // 3) // (RG_BLOCKS * 16) * (RG_BLOCKS * 16)
RG_BS = D_C // RG_BLOCKS
CONV_W = 4
CONV_LEFT = 2
RGLRU_C = 8.0
SCAN_DIRECTIONS = (False, True)

A_SHIFTED = 3 * D_A + 2 * R_W + 2 * R_A
EV_IN = A_SHIFTED + D_A + 4 * D_B
EV_SPLITS = (A_SHIFTED, A_SHIFTED + D_A, A_SHIFTED + D_A + D_B, A_SHIFTED + D_A + 2 * D_B, A_SHIFTED + D_A + 3 * D_B)
RW_SPLITS = (D_A, 2 * D_A, 3 * D_A, 3 * D_A + 2 * R_W)

kernel_name = "hybrid_rwkv7_natten_rglru_prefix_dit"


def rmsnorm(x, g):
    xf = x.astype(jnp.float32)
    y = xf * lax.rsqrt(jnp.mean(xf * xf, axis=-1, keepdims=True) + EPS_RMS)
    return (y * g).astype(x.dtype)


def heads(t, dh):
    return t.reshape(t.shape[0], t.shape[1], -1, dh)


def token_shift(f, mu):
    prev = jnp.pad(f[:, :-1], ((0, 0), (1, 0), (0, 0)))
    nxt = jnp.pad(f[:, 1:], ((0, 0), (0, 1), (0, 0)))
    return f + mu[0] * (prev - f) + mu[1] * (nxt - f)


def rwkv7_prepare(f, mu, w0, w_up, a0, a_up, k_k, k_a):
    f = token_shift(f.astype(jnp.float32), mu)
    r, k, v, cw, ca = jnp.split(f, RW_SPLITS, axis=-1)
    B, T = f.shape[:2]
    cw = cw.reshape(B, T, 2, R_W)
    ca = ca.reshape(B, T, 2, R_A)
    w_log = -jax.nn.softplus(-(w0 + jnp.einsum('btdr,drc->btdc', jnp.tanh(cw), w_up))) - 0.5
    decay = jnp.exp(-jnp.exp(w_log))
    a = jax.nn.sigmoid(a0 + jnp.einsum('btdr,drc->btdc', ca, a_up))
    kk = heads(k * k_k, HEAD_A)
    kk = kk * lax.rsqrt(jnp.sum(kk * kk, axis=-1, keepdims=True) + 1e-12)
    k_dir = k[:, :, None] * (1.0 + (a - 1.0) * k_a)
    split_heads = lambda t: t.reshape(B, T, 2, H_A, HEAD_A)
    return (heads(r, HEAD_A), heads(v, HEAD_A), kk, split_heads(decay), split_heads(k_dir), split_heads(a))


def rwkv7_scan(S0, r, decay, k, v, kk, a, reverse):
    def step(S, inp):
        r_t, w_t, k_t, v_t, kk_t, a_t = inp
        S = (S * w_t[:, :, None, :]
             - jnp.einsum('bhvk,bhk->bhv', S, kk_t)[..., None] * (kk_t * a_t)[:, :, None, :]
             + v_t[..., None] * k_t[:, :, None, :])
        return S, jnp.einsum('bhvk,bhk->bhv', S, r_t)
    xs = tuple(jnp.moveaxis(t, 1, 0) for t in (r, decay, k, v, kk, a))
    S, o = lax.scan(step, S0, xs, reverse=reverse)
    return S, jnp.moveaxis(o, 0, 1)


def rwkv7_readout(o, r, k_dir, v, r_k, gn_w, gn_b, g):
    B, T = o.shape[:2]
    mean = jnp.mean(o, axis=-1, keepdims=True)
    var = jnp.mean(jnp.square(o - mean), axis=-1, keepdims=True)
    on = ((o - mean) * lax.rsqrt(var + GN_EPS)).reshape(B, T, D_A) * gn_w + gn_b
    bonus = jnp.einsum('bthn,btdhn,hn->bth', r, k_dir, r_k)[..., None] * v
    return ((on + bonus.reshape(B, T, D_A)) * jax.nn.silu(g.astype(jnp.float32))).astype(g.dtype)


def rwkv7_mixer(f, f_c, g, g_c, mu, w0, w_up, a0, a_up, k_k, k_a, r_k, gn_w, gn_b, ctx_out):
    r, v, kk, decay, k_dir, a = rwkv7_prepare(f, mu, w0, w_up, a0, a_up, k_k, k_a)
    r_c, v_c, kk_c, decay_c, k_dir_c, a_c = rwkv7_prepare(f_c, mu, w0, w_up, a0, a_up, k_k, k_a)
    B = r.shape[0]
    outs, outs_c = [], []
    for d, rev in enumerate(SCAN_DIRECTIONS):
        S0 = jnp.zeros((B, H_A, HEAD_A, HEAD_A), jnp.float32)
        S_c, o_c = rwkv7_scan(S0, r_c, decay_c[:, :, d], k_dir_c[:, :, d], v_c, kk_c, a_c[:, :, d], rev)
        _, o = rwkv7_scan(S_c, r, decay[:, :, d], k_dir[:, :, d], v, kk, a[:, :, d], rev)
        outs.append(o)
        outs_c.append(o_c)
    y = rwkv7_readout(outs[0] + outs[1], r, k_dir, v, r_k, gn_w, gn_b, g)
    y_c = rwkv7_readout(outs_c[0] + outs_c[1], r_c, k_dir_c, v_c, r_k, gn_w, gn_b, g_c) if ctx_out else None
    return y, y_c


def neighbourhood_attention(q, k, v, kc, vc, rpb):
    B, S, H, Dh = q.shape
    rows = S // GRID_W
    kh = min(NA_WIN_H, rows)
    n_loc = kh * NA_WIN_W
    scale = Dh ** -0.5
    qg = q.reshape(B, rows, GRID_W, H, Dh)
    kg = k.reshape(B, rows, GRID_W, H, Dh)
    vg = v.reshape(B, rows, GRID_W, H, Dh)
    cols = jnp.arange(GRID_W)
    col_start = jnp.clip(cols - NA_WIN_W // 2, 0, GRID_W - NA_WIN_W)
    col_idx = col_start[:, None] + jnp.arange(NA_WIN_W)[None, :]
    dx = col_idx - cols[:, None] + (NA_WIN_W - 1)

    def row_block(y):
        y0 = jnp.clip(y - kh // 2, 0, rows - kh)
        k_nb = lax.dynamic_slice_in_dim(kg, y0, kh, axis=1)[:, :, col_idx]
        v_nb = lax.dynamic_slice_in_dim(vg, y0, kh, axis=1)[:, :, col_idx]
        qy = lax.dynamic_index_in_dim(qg, y, axis=1, keepdims=False)
        dy = y0 + jnp.arange(kh) - y + (NA_WIN_H - 1)
        bias = rpb[:, dy[None, :, None], dx[:, None, :]].reshape(H, GRID_W, n_loc)
        s_loc = jnp.einsum('bwhd,brwkhd->bhwrk', qy, k_nb).reshape(B, H, GRID_W, n_loc)
        s_ctx = jnp.einsum('bwhd,bchd->bhwc', qy, kc)
        s = jnp.concatenate([s_loc.astype(jnp.float32) * scale + bias.astype(jnp.float32),
                             s_ctx.astype(jnp.float32) * scale], axis=-1)
        p = jax.nn.softmax(s, axis=-1).astype(v.dtype)
        p_loc = p[..., :n_loc].reshape(B, H, GRID_W, kh, NA_WIN_W)
        return (jnp.einsum('bhwrk,brwkhd->bwhd', p_loc, v_nb)
                + jnp.einsum('bhwc,bchd->bwhd', p[..., n_loc:], vc))

    o = lax.map(row_block, jnp.arange(rows))
    return jnp.moveaxis(o, 0, 1).reshape(B, S, H, Dh)


def ctx_attention(qc, kc, vc):
    s = jnp.einsum('bqhd,bkhd->bhqk', qc, kc).astype(jnp.float32) * (qc.shape[-1] ** -0.5)
    p = jax.nn.softmax(s, axis=-1).astype(vc.dtype)
    return jnp.einsum('bhqk,bkhd->bqhd', p, vc)


def even_mixer(h, hc, w_in, mu, w0, w_up, a0, a_up, k_k, k_a, r_k, gn_w, gn_b, rpb, ctx_out):
    fa, ga, qb, kb, vb, gb = jnp.split(h @ w_in, EV_SPLITS, axis=-1)
    fa_c, ga_c, qb_c, kb_c, vb_c, gb_c = jnp.split(hc @ w_in, EV_SPLITS, axis=-1)
    ya, ya_c = rwkv7_mixer(fa, fa_c, ga, ga_c, mu, w0, w_up, a0, a_up, k_k, k_a, r_k, gn_w, gn_b, ctx_out)
    kc, vc = heads(kb_c, HEAD_B), heads(vb_c, HEAD_B)
    ob = neighbourhood_attention(heads(qb, HEAD_B), heads(kb, HEAD_B), heads(vb, HEAD_B), kc, vc, rpb)
    yb = ob.reshape(h.shape[0], h.shape[1], D_B) * jax.nn.silu(gb)
    y = jnp.concatenate([ya, yb.astype(ya.dtype)], axis=-1)
    if not ctx_out:
        return y, None
    ob_c = ctx_attention(heads(qb_c, HEAD_B), kc, vc)
    yb_c = ob_c.reshape(hc.shape[0], hc.shape[1], D_B) * jax.nn.silu(gb_c)
    return y, jnp.concatenate([ya_c, yb_c.astype(ya_c.dtype)], axis=-1)


def centred_depthwise_conv(x, w, b):
    out = lax.conv_general_dilated(x, w[:, None, :].astype(x.dtype), window_strides=(1,),
                                   padding=[(CONV_LEFT, CONV_W - 1 - CONV_LEFT)],
                                   dimension_numbers=('NWC', 'WIO', 'NWC'),
                                   feature_group_count=x.shape[-1])
    return out + b


def rglru_coeffs(u, wa, ba, wx, bx, lam):
    B, T, _ = u.shape
    ub = u.reshape(B, T, RG_BLOCKS, RG_BS)
    gate_r = jax.nn.sigmoid(jnp.einsum('bthi,hij->bthj', ub, wa).reshape(B, T, D_C) + ba)
    gate_i = jax.nn.sigmoid(jnp.einsum('bthi,hij->bthj', ub, wx).reshape(B, T, D_C) + bx)
    log_a = -RGLRU_C * gate_r * jax.nn.softplus(-lam)
    a = jnp.exp(log_a)
    b = jnp.sqrt(-jnp.expm1(2.0 * log_a)) * (gate_i * u)
    return a, b


def linear_scan(a, b, h0, reverse):
    def combine(e1, e2):
        a1, b1 = e1
        a2, b2 = e2
        return a1 * a2, a2 * b1 + b2
    a_cum, h = lax.associative_scan(combine, (a, b), reverse=reverse, axis=1)
    return h + a_cum * h0[:, None]


def rglru_mixer(h, hc, w_in, conv_w, conv_b, ga_w, ga_b, gx_w, gx_b, lam, ctx_out):
    xr, g = jnp.split(h @ w_in, 2, axis=-1)
    xr_c, g_c = jnp.split(hc @ w_in, 2, axis=-1)
    u = centred_depthwise_conv(xr, conv_w, conv_b).astype(jnp.float32)
    u_c = centred_depthwise_conv(xr_c, conv_w, conv_b).astype(jnp.float32)
    ys, ys_c = [], []
    for d, rev in enumerate(SCAN_DIRECTIONS):
        a_c, b_c = rglru_coeffs(u_c, ga_w[d], ga_b[d], gx_w[d], gx_b[d], lam[d])
        h_c = linear_scan(a_c, b_c, jnp.zeros_like(u_c[:, 0]), rev)
        h0 = h_c[:, 0] if rev else h_c[:, -1]
        a, b = rglru_coeffs(u, ga_w[d], ga_b[d], gx_w[d], gx_b[d], lam[d])
        ys.append(linear_scan(a, b, h0, rev))
        ys_c.append(h_c)
    y = ((ys[0] + ys[1]) * jax.nn.silu(g.astype(jnp.float32))).astype(h.dtype)
    if not ctx_out:
        return y, None
    y_c = ((ys_c[0] + ys_c[1]) * jax.nn.silu(g_c.astype(jnp.float32))).astype(hc.dtype)
    return y, y_c


def setup_inputs(seed: int = 0) -> dict:
    key = jax.random.key(seed)
    ks = iter(jax.random.split(key, 40))
    nrm = lambda shape, s: s * jax.random.normal(next(ks), shape, jnp.float32)
    uni = lambda shape, lo, hi: jax.random.uniform(next(ks), shape, jnp.float32, lo, hi)
    D = D_MODEL
    lam_u = uni((N_ODD, 2, D_C), 0.9, 0.999) ** (1.0 / RGLRU_C)
    return {
        "x": nrm((BATCH, SEQ, D), 1.0),
        "c": nrm((BATCH, D), 1.0),
        "ctx": nrm((BATCH, CTX_LEN, D), 1.0),
        "c_ctx": nrm((D,), 1.0),
        "mod_w": nrm((DEPTH, D, 3 * D), 0.5 * D ** -0.5),
        "mod_b": nrm((DEPTH, 3 * D), 0.02),
        "norm_pre": 1.0 + nrm((DEPTH, D), 0.05),
        "norm_post": 1.0 + nrm((DEPTH, D), 0.05),
        "ev_w_in": nrm((N_EVEN, D, EV_IN), D ** -0.5),
        "ev_mu": uni((N_EVEN, 2, A_SHIFTED), 0.0, 0.5),
        "ev_w0": jnp.linspace(-6.0, -0.5, D_A)[None, None, :] + nrm((N_EVEN, 2, D_A), 0.3),
        "ev_w_up": nrm((N_EVEN, 2, R_W, D_A), 0.5 * R_W ** -0.5),
        "ev_a0": nrm((N_EVEN, 2, D_A), 0.3),
        "ev_a_up": nrm((N_EVEN, 2, R_A, D_A), 0.3 * R_A ** -0.5),
        "ev_k_k": 0.85 + nrm((N_EVEN, D_A), 0.05),
        "ev_k_a": 1.0 + nrm((N_EVEN, D_A), 0.05),
        "ev_r_k": nrm((N_EVEN, H_A, HEAD_A), 0.1),
        "ev_gn_w": 1.0 + nrm((N_EVEN, D_A), 0.05),
        "ev_gn_b": nrm((N_EVEN, D_A), 0.02),
        "ev_rpb": nrm((N_EVEN, H_B, 2 * NA_WIN_H - 1, 2 * NA_WIN_W - 1), 0.5),
        "ev_w_out": nrm((N_EVEN, D_A + D_B, D), (D_A + D_B) ** -0.5),
        "od_w_in": nrm((N_ODD, D, 2 * D_C), D ** -0.5),
        "od_conv_w": nrm((N_ODD, CONV_W, D_C), CONV_W ** -0.5),
        "od_conv_b": nrm((N_ODD, D_C), 0.02),
        "od_gate_a_w": nrm((N_ODD, 2, RG_BLOCKS, RG_BS, RG_BS), RG_BS ** -0.5),
        "od_gate_a_b": nrm((N_ODD, 2, D_C), 0.02),
        "od_gate_x_w": nrm((N_ODD, 2, RG_BLOCKS, RG_BS, RG_BS), RG_BS ** -0.5),
        "od_gate_x_b": nrm((N_ODD, 2, D_C), 0.02),
        "od_lambda": jnp.log(lam_u) - jnp.log1p(-lam_u),
        "od_w_out": nrm((N_ODD, D_C, D), D_C ** -0.5),
    }


def reference(x, c, ctx, c_ctx, mod_w, mod_b, norm_pre, norm_post, ev_w_in, ev_mu, ev_w0, ev_w_up,
              ev_a0, ev_a_up, ev_k_k, ev_k_a, ev_r_k, ev_gn_w, ev_gn_b, ev_rpb, ev_w_out, od_w_in,
              od_conv_w, od_conv_b, od_gate_a_w, od_gate_a_b, od_gate_x_w, od_gate_x_b, od_lambda,
              od_w_out):
    xc = ctx
    for layer in range(DEPTH):
        ctx_out = layer < DEPTH - 1
        i = layer // 2
        shift, scale, gate = jnp.split(jax.nn.silu(c) @ mod_w[layer] + mod_b[layer], 3, axis=-1)
        shift_c, scale_c, gate_c = jnp.split(jax.nn.silu(c_ctx) @ mod_w[layer] + mod_b[layer], 3, axis=-1)
        h = rmsnorm(x, norm_pre[layer]) * (1.0 + scale[:, None]) + shift[:, None]
        hc = rmsnorm(xc, norm_pre[layer]) * (1.0 + scale_c) + shift_c
        if layer % 2 == 0:
            y, yc = even_mixer(h, hc, ev_w_in[i], ev_mu[i], ev_w0[i], ev_w_up[i], ev_a0[i], ev_a_up[i],
                               ev_k_k[i], ev_k_a[i], ev_r_k[i], ev_gn_w[i], ev_gn_b[i], ev_rpb[i], ctx_out)
            w_out = ev_w_out[i]
        else:
            y, yc = rglru_mixer(h, hc, od_w_in[i], od_conv_w[i], od_conv_b[i], od_gate_a_w[i], od_gate_a_b[i],
                                od_gate_x_w[i], od_gate_x_b[i], od_lambda[i], ctx_out)
            w_out = od_w_out[i]
        x = x + gate[:, None] * rmsnorm(y @ w_out, norm_post[layer])
        if ctx_out:
            xc = xc + gate_c * rmsnorm(yc @ w_out, norm_post[layer])
    return x
```

```python
import functools
import math

import jax
import jax.numpy as jnp
import numpy as np
from jax import lax
from jax.experimental import pallas as pl
from jax.experimental.pallas import tpu as pltpu

F32 = jnp.float32
BF16 = jnp.bfloat16

EPS_RMS = 1e-6
GN_EPS = 64e-5
RGLRU_C = 8.0
HEAD = 64
GRID_W = 64
NA_WIN_H = 8
NA_WIN_W = 16
CONV_W = 4
CONV_LEFT = 2
LANES = 128
SUBLANES = 8
CHUNK = 64
NEG_BIG = -1e30
VMEM_LIMIT = 56 * 1024 * 1024


def _cparams(sem):
    return pltpu.CompilerParams(dimension_semantics=sem, vmem_limit_bytes=VMEM_LIMIT)


def _pick(n, cands):
    for c in cands:
        if n % c == 0:
            return c
    raise ValueError(f"no tile for {n} in {cands}")


def _bdot(a, b):
    return jnp.dot(a.astype(BF16), b.astype(BF16), preferred_element_type=F32)


def _bdot_nt(a, b):
    return lax.dot_general(a.astype(BF16), b.astype(BF16), (((1,), (1,)), ((), ())),
                           preferred_element_type=F32)


def _bdot_tn(a, b):
    return lax.dot_general(a.astype(BF16), b.astype(BF16), (((0,), (0,)), ((), ())),
                           preferred_element_type=F32)


def _split3(x):
    hi = x.astype(BF16)
    r1 = x - hi.astype(F32)
    mid = r1.astype(BF16)
    lo = (r1 - mid.astype(F32)).astype(BF16)
    return hi, mid, lo


def _dot_sel_r(x, sel):
    hi, mid, lo = _split3(x)
    d = lambda t: jnp.dot(t, sel, preferred_element_type=F32)
    return d(hi) + d(mid) + d(lo)


def _dot_sel_l(sel, x):
    hi, mid, lo = _split3(x)
    d = lambda t: jnp.dot(sel, t, preferred_element_type=F32)
    return d(hi) + d(mid) + d(lo)


def _sigmoid(x):
    return 1.0 / (1.0 + jnp.exp(-x))


def _silu(x):
    return x * _sigmoid(x)


def _mod_kernel(c_ref, w_ref, b_ref, o_ref):
    cs = _silu(c_ref[...])
    o_ref[...] = _bdot(cs, w_ref[...]) + b_ref[...]


def _modulation(c8, mod_w, mod_b):
    depth, d, n = mod_w.shape
    tn = _pick(n, (1024, 512, 256, 128))
    return pl.pallas_call(
        _mod_kernel,
        out_shape=jax.ShapeDtypeStruct((depth, SUBLANES, n), F32),
        grid=(depth, n // tn),
        in_specs=[
            pl.BlockSpec((SUBLANES, d), lambda l, j: (0, 0)),
            pl.BlockSpec((None, d, tn), lambda l, j: (l, 0, j)),
            pl.BlockSpec((None, 1, tn), lambda l, j: (l, 0, j)),
        ],
        out_specs=pl.BlockSpec((None, SUBLANES, tn), lambda l, j: (l, 0, j)),
        compiler_params=_cparams(("parallel", "parallel")),
        name="modulation",
    )(c8, mod_w, mod_b.reshape(depth, 1, n))


def _inproj_kernel(x_ref, mod_ref, g_ref, w_ref, o_ref, h_ref, *, d):
    @pl.when(pl.program_id(1) == 0)
    def _():
        x = x_ref[...]
        ms = jnp.mean(x * x, axis=-1, keepdims=True)
        y = x * lax.rsqrt(ms + EPS_RMS) * g_ref[...]
        h_ref[...] = (y * (1.0 + mod_ref[:, d:2 * d]) + mod_ref[:, 0:d]).astype(BF16)

    o_ref[...] = jnp.dot(h_ref[...], w_ref[...], preferred_element_type=F32)


def _inproj(x_all, mods_l, g, w_bf, tm, mod_row):
    m, d = x_all.shape
    n = w_bf.shape[1]
    tn = _pick(n, (1664, 1408, 1280, 1024, 768, 640, 512, 384, 256, 128))
    return pl.pallas_call(
        functools.partial(_inproj_kernel, d=d),
        out_shape=jax.ShapeDtypeStruct((m, n), F32),
        grid=(m // tm, n // tn),
        in_specs=[
            pl.BlockSpec((tm, d), lambda i, j: (i, 0)),
            pl.BlockSpec((None, 1, 3 * d), lambda i, j: (mod_row(i), 0, 0)),
            pl.BlockSpec((1, d), lambda i, j: (0, 0)),
            pl.BlockSpec((d, tn), lambda i, j: (0, j)),
        ],
        out_specs=pl.BlockSpec((tm, tn), lambda i, j: (i, j)),
        scratch_shapes=[pltpu.VMEM((tm, d), BF16)],
        compiler_params=_cparams(("parallel", "arbitrary")),
        name="inproj",
    )(x_all, mods_l, g, w_bf)


def _outproj_kernel(*refs, n_in, d):
    y_refs = refs[:n_in]
    w_refs = refs[n_in:2 * n_in]
    x_ref, mod_ref, g_ref, o_ref = refs[2 * n_in:]
    acc = jnp.dot(y_refs[0][...], w_refs[0][...], preferred_element_type=F32)
    for i in range(1, n_in):
        acc = acc + jnp.dot(y_refs[i][...], w_refs[i][...], preferred_element_type=F32)
    ms = jnp.mean(acc * acc, axis=-1, keepdims=True)
    z = acc * lax.rsqrt(ms + EPS_RMS) * g_ref[...]
    o_ref[...] = x_ref[...] + mod_ref[:, 2 * d:3 * d] * z


def _outproj(ys, ws, x_all, mods_l, g, tm, mod_row, m_out):
    d = x_all.shape[1]
    n_in = len(ys)
    in_specs = [pl.BlockSpec((tm, y.shape[1]), lambda i: (i, 0)) for y in ys]
    in_specs += [pl.BlockSpec(w.shape, lambda i: (0, 0)) for w in ws]
    in_specs += [
        pl.BlockSpec((tm, d), lambda i: (i, 0)),
        pl.BlockSpec((None, 1, 3 * d), lambda i: (mod_row(i), 0, 0)),
        pl.BlockSpec((1, d), lambda i: (0, 0)),
    ]
    return pl.pallas_call(
        functools.partial(_outproj_kernel, n_in=n_in, d=d),
        out_shape=jax.ShapeDtypeStruct((m_out, d), F32),
        grid=(m_out // tm,),
        in_specs=in_specs,
        out_specs=pl.BlockSpec((tm, d), lambda i: (i, 0)),
        compiler_params=_cparams(("parallel",)),
        name="outproj",
    )(*ys, *ws, x_all, mods_l, g)


class _Geom:
    def __init__(self, batch, seq, ctx):
        self.batch, self.seq, self.ctx = batch, seq, ctx
        self.n_lat = batch * seq
        self.m = batch * (seq + ctx)

    def seq_edges(self, row0, rows):
        in_lat = row0 < self.n_lat
        off = jnp.where(in_lat, row0 % self.seq, (row0 - self.n_lat) % self.ctx)
        length = jnp.where(in_lat, self.seq, self.ctx)
        return off == 0, off + rows == length

    def chunk_block(self, b, c, rows, reverse):
        cc = self.ctx // rows
        sc = self.seq // rows
        if reverse:
            ctx_blk = self.n_lat // rows + b * cc + (cc - 1 - c)
            lat_blk = b * sc + (sc - 1 - (c - cc))
        else:
            ctx_blk = self.n_lat // rows + b * cc + c
            lat_blk = b * sc + (c - cc)
        return jnp.where(c < cc, ctx_blk, lat_blk)


def _shift_rows(x, hp_ref, hn_ref, cols, is_first, is_last, back, fwd):
    rows = x.shape[0]
    ridx = lax.broadcasted_iota(jnp.int32, (rows, 1), 0)
    outs = {}
    for s in range(1, back + 1):
        y = pltpu.roll(x, s, 0)
        for j in range(s):
            halo = hp_ref[SUBLANES - s + j:SUBLANES - s + j + 1, cols]
            halo = jnp.where(is_first, 0.0, halo)
            y = jnp.where(ridx == j, halo, y)
        outs[-s] = y
    for s in range(1, fwd + 1):
        y = pltpu.roll(x, rows - s, 0)
        for j in range(s):
            halo = hn_ref[j:j + 1, cols]
            halo = jnp.where(is_last, 0.0, halo)
            y = jnp.where(ridx == rows - s + j, halo, y)
        outs[s] = y
    return outs


def _halo_specs(tp, width, m):
    nb8 = m // SUBLANES
    r8 = tp // SUBLANES
    return [
        pl.BlockSpec((SUBLANES, width), lambda i: (jnp.maximum(i * r8 - 1, 0), 0)),
        pl.BlockSpec((SUBLANES, width), lambda i: (jnp.minimum((i + 1) * r8, nb8 - 1), 0)),
    ]


def _rwkv_prep_kernel(f_ref, hp_ref, hn_ref, mu_ref, w0_ref, wup_ref, a0_ref, aup_ref, kk_ref, ka_ref,
                      sel_ref, selt_ref, r_ref, v_ref, kap_ref, kd_ref, bb_ref, lw_ref, *, geom, tp, da):
    i = pl.program_id(0)
    is_first, is_last = geom.seq_edges(i * tp, tp)

    def shifted(lo, hi):
        cols = slice(lo, hi)
        x = f_ref[:, cols]
        nb = _shift_rows(x, hp_ref, hn_ref, cols, is_first, is_last, 1, 1)
        return x + mu_ref[0:1, cols] * (nb[-1] - x) + mu_ref[1:2, cols] * (nb[1] - x)

    r = shifted(0, da)
    k = shifted(da, 2 * da)
    v = shifted(2 * da, 3 * da)
    cw = jnp.tanh(shifted(3 * da, 3 * da + LANES))
    ca = shifted(3 * da + LANES, 3 * da + 2 * LANES)
    r_ref[...] = r
    v_ref[...] = v
    kk = k * kk_ref[...]
    ss = _dot_sel_r(kk * kk, sel_ref[...])
    inv = lax.rsqrt(ss + 1e-12)
    kap = kk * _dot_sel_r(inv, selt_ref[...])
    kap_ref[...] = kap
    for dd in range(2):
        zw = w0_ref[dd:dd + 1, :] + _bdot(cw, wup_ref[dd])
        lw_ref[dd] = -math.exp(-0.5) * _sigmoid(zw)
        a = _sigmoid(a0_ref[dd:dd + 1, :] + _bdot(ca, aup_ref[dd]))
        kd_ref[dd] = k * (1.0 + (a - 1.0) * ka_ref[...])
        bb_ref[dd] = kap * a


def _rwkv_prep(proj_a, geom, tp, mu, w0, wup_pad, a0, aup_pad, k_k, k_a, sel, selt):
    m, wa = proj_a.shape
    da = k_k.shape[1]
    full = lambda a: pl.BlockSpec(a.shape, lambda i: (0,) * a.ndim)
    tok = jax.ShapeDtypeStruct((m, da), F32)
    tok2 = jax.ShapeDtypeStruct((2, m, da), F32)
    params = (mu, w0, wup_pad, a0, aup_pad, k_k, k_a, sel, selt)
    return pl.pallas_call(
        functools.partial(_rwkv_prep_kernel, geom=geom, tp=tp, da=da),
        out_shape=(tok, tok, tok, tok2, tok2, tok2),
        grid=(m // tp,),
        in_specs=[pl.BlockSpec((tp, wa), lambda i: (i, 0))] + _halo_specs(tp, wa, m) + [full(p) for p in params],
        out_specs=(
            pl.BlockSpec((tp, da), lambda i: (i, 0)),
            pl.BlockSpec((tp, da), lambda i: (i, 0)),
            pl.BlockSpec((tp, da), lambda i: (i, 0)),
            pl.BlockSpec((2, tp, da), lambda i: (0, i, 0)),
            pl.BlockSpec((2, tp, da), lambda i: (0, i, 0)),
            pl.BlockSpec((2, tp, da), lambda i: (0, i, 0)),
        ),
        compiler_params=_cparams(("parallel",)),
        name="rwkv_prep",
    )(proj_a, proj_a, proj_a, *params)


def _rwkv_chunk(s_prev, r, v, kap, kd, bb, lw, consts):
    tri_sel, bd_strict, row_incl, bd_full, m0, m1, eye, last_row, levels = consts
    L = CHUNK
    cum = _dot_sel_l(tri_sel, lw)
    tot = cum[last_row:last_row + 1, :]
    e_pos = jnp.exp(cum)
    e_neg = jnp.exp(-cum)
    rt = r * e_pos
    kt = kd * e_neg
    bt = bb * e_neg
    kapt = kap * jnp.exp(cum - lw)
    stack = lambda x: jnp.concatenate([x * m0, x * m1], axis=0)
    kap_s = stack(kapt)
    vv_s = stack(v)
    g = _bdot_nt(jnp.concatenate([kap_s, rt], axis=0), jnp.concatenate([stack(bt), stack(kt)], axis=0))
    n = jnp.where(bd_strict, -g[:2 * L, :2 * L], 0.0)
    mk = jnp.where(bd_strict, g[:2 * L, 2 * L:], 0.0)
    arb = jnp.where(row_incl, g[2 * L:, :2 * L], 0.0)
    ark = jnp.where(row_incl, g[2 * L:, 2 * L:], 0.0)
    tm = eye + jnp.where(levels[0], n, 0.0)
    for lvl in levels[1:]:
        tm = tm + _bdot(tm, _bdot(jnp.where(lvl, n, 0.0), tm))
    wy = _bdot(tm, jnp.concatenate([kap_s, _bdot(mk, vv_s)], axis=1))
    w = wy[:L, :LANES] + wy[L:, :LANES]
    y = wy[:L, LANES:] + wy[L:, LANES:]
    z = _bdot(ark, vv_s)
    rest = jnp.exp(tot - cum)
    gk = _bdot_tn(v, kd * rest)
    wr = _bdot_nt(jnp.concatenate([w, rt], axis=0), s_prev)
    u = -(wr[:L] + y)
    o = wr[L:] + _bdot(arb, stack(u)) + z
    s_new = jnp.where(bd_full, s_prev * jnp.exp(tot) + _bdot_tn(u, bb * rest) + gk, 0.0)
    return s_new, o


def _rwkv_consts(reverse):
    L = CHUNK
    ri = lax.broadcasted_iota(jnp.int32, (2 * L, 2 * L), 0)
    ci = lax.broadcasted_iota(jnp.int32, (2 * L, 2 * L), 1)
    same = (ri >= L) == (ci >= L)
    rt, ct = ri & (L - 1), ci & (L - 1)
    bd_strict = same & ((ct > rt) if reverse else (ct < rt))
    r1 = lax.broadcasted_iota(jnp.int32, (L, 2 * L), 0)
    c1 = lax.broadcasted_iota(jnp.int32, (L, 2 * L), 1) & (L - 1)
    row_incl = (c1 >= r1) if reverse else (c1 <= r1)
    r2 = lax.broadcasted_iota(jnp.int32, (L, L), 0)
    c2 = lax.broadcasted_iota(jnp.int32, (L, L), 1)
    tri_sel = jnp.where((c2 >= r2) if reverse else (c2 <= r2), 1.0, 0.0).astype(BF16)
    lane = lax.broadcasted_iota(jnp.int32, (1, LANES), 1)
    m0 = jnp.where(lane < HEAD, 1.0, 0.0)
    m1 = 1.0 - m0
    eye = jnp.where(ri == ci, 1.0, 0.0)
    levels = tuple(((ri >> (l + 1)) == (ci >> (l + 1))) & ((ri >> l) != (ci >> l))
                   for l in range(int(math.log2(L))))
    return (tri_sel, bd_strict, row_incl, same, m0, m1, eye, 0 if reverse else L - 1, levels)


def _rwkv_scan_kernel(*refs, npairs):
    ins = refs[:12]
    of_ref, ob_ref, s_ref = refs[12:]

    @pl.when(pl.program_id(2) == 0)
    def _():
        s_ref[...] = jnp.zeros(s_ref.shape, F32)

    for dd, o_ref in ((0, of_ref), (1, ob_ref)):
        consts = _rwkv_consts(dd == 1)
        r_ref, v_ref, kap_ref, kd_ref, bb_ref, lw_ref = ins[6 * dd:6 * dd + 6]
        for p in range(npairs):
            sl = slice(p * LANES, (p + 1) * LANES)
            s_new, o = _rwkv_chunk(s_ref[dd, p], r_ref[:, sl], v_ref[:, sl], kap_ref[:, sl],
                                   kd_ref[:, sl], bb_ref[:, sl], lw_ref[:, sl], consts)
            s_ref[dd, p] = s_new
            o_ref[:, sl] = o


def _rwkv_scan(r, v, kap, kd, bb, lw, geom):
    m, da = r.shape
    gw = _pick(da, (512, 256, 128))
    npairs = gw // LANES
    nch = (geom.seq + geom.ctx) // CHUNK
    in_specs = []
    for dd in range(2):
        blk = lambda b, g, c, dd=dd: geom.chunk_block(b, c, CHUNK, dd == 1)
        tok = pl.BlockSpec((CHUNK, gw), lambda b, g, c, blk=blk: (blk(b, g, c), g))
        tokd = pl.BlockSpec((None, CHUNK, gw), lambda b, g, c, blk=blk, dd=dd: (dd, blk(b, g, c), g))
        in_specs += [tok, tok, tok, tokd, tokd, tokd]
    out_specs = tuple(
        pl.BlockSpec((CHUNK, gw), lambda b, g, c, dd=dd: (geom.chunk_block(b, c, CHUNK, dd == 1), g))
        for dd in range(2))
    tok_shape = jax.ShapeDtypeStruct((m, da), F32)
    return pl.pallas_call(
        functools.partial(_rwkv_scan_kernel, npairs=npairs),
        out_shape=(tok_shape, tok_shape),
        grid=(geom.batch, da // gw, nch),
        in_specs=in_specs,
        out_specs=out_specs,
        scratch_shapes=[pltpu.VMEM((2, npairs, LANES, LANES), F32)],
        compiler_params=_cparams(("parallel", "parallel", "arbitrary")),
        name="rwkv_scan",
    )(r, v, kap, kd, bb, lw, r, v, kap, kd, bb, lw)


def _rwkv_readout_kernel(of_ref, ob_ref, r_ref, v_ref, kd_ref, g_ref, rk_ref, gw_ref, gb_ref,
                         sel_ref, selt_ref, y_ref):
    sel, selt = sel_ref[...], selt_ref[...]
    head_mean = lambda x: _dot_sel_r(_dot_sel_r(x, sel) * (1.0 / HEAD), selt)
    o = of_ref[...] + ob_ref[...]
    cen = o - head_mean(o)
    var = head_mean(cen * cen)
    on = cen * lax.rsqrt(var + GN_EPS) * gw_ref[...] + gb_ref[...]
    bonus = _dot_sel_r(_dot_sel_r(r_ref[...] * (kd_ref[0] + kd_ref[1]) * rk_ref[...], sel), selt)
    y_ref[...] = ((on + bonus * v_ref[...]) * _silu(g_ref[...])).astype(y_ref.dtype)


def _rwkv_readout(o_f, o_b, r, v, kd, proj_b, r_k, gn_w, gn_b, sel, selt, tr):
    m, da = r.shape
    full = lambda a: pl.BlockSpec(a.shape, lambda i: (0,) * a.ndim)
    tok = pl.BlockSpec((tr, da), lambda i: (i, 0))
    params = (r_k, gn_w, gn_b, sel, selt)
    return pl.pallas_call(
        _rwkv_readout_kernel,
        out_shape=jax.ShapeDtypeStruct((m, da), BF16),
        grid=(m // tr,),
        in_specs=[tok, tok, tok, tok, pl.BlockSpec((2, tr, da), lambda i: (0, i, 0)), tok] + [full(p) for p in params],
        out_specs=tok,
        compiler_params=_cparams(("parallel",)),
        name="rwkv_readout",
    )(o_f, o_b, r, v, kd, proj_b, *params)


def _attn_core(q, k_loc, v_loc, bias, kc, vc, m0):
    n = q.shape[0]
    qs = jnp.concatenate([q * m0, q * (1.0 - m0)], axis=0) * (HEAD ** -0.5)
    s_ctx = _bdot_nt(qs, kc)
    mx = jnp.max(s_ctx, axis=-1, keepdims=True)
    if k_loc is not None:
        s_loc = _bdot_nt(qs, k_loc) + bias
        mx = jnp.maximum(mx, jnp.max(s_loc, axis=-1, keepdims=True))
        p_loc = jnp.exp(s_loc - mx)
    p_ctx = jnp.exp(s_ctx - mx)
    den = jnp.sum(p_ctx, axis=-1, keepdims=True)
    acc = _bdot(p_ctx, vc)
    if k_loc is not None:
        den = den + jnp.sum(p_loc, axis=-1, keepdims=True)
        acc = acc + _bdot(p_loc, v_loc)
    acc = acc / den
    return jnp.where(m0 > 0.5, acc[:n], acc[n:])


def _na_kernel(q_ref, g_ref, k_ref, v_ref, kc_ref, vc_ref, bias_ref, y_ref, *, rows_per_step, img_rows, n_lat_steps):
    step = pl.program_id(2)
    lane = lax.broadcasted_iota(jnp.int32, (1, LANES), 1)
    m0 = jnp.where(lane < HEAD, 1.0, 0.0)
    kc = kc_ref[...]
    vc = vc_ref[...]
    kh = min(NA_WIN_H, img_rows)

    @pl.when(step < n_lat_steps)
    def _():
        for yy in range(rows_per_step):
            y = step * rows_per_step + yy
            y0 = jnp.clip(y - kh // 2, 0, img_rows - kh)
            start = pl.multiple_of(y0 * GRID_W, GRID_W)
            k_loc = k_ref[pl.ds(start, kh * GRID_W), :]
            v_loc = v_ref[pl.ds(start, kh * GRID_W), :]
            bias = bias_ref[:, y - y0].reshape(2 * GRID_W, kh * GRID_W)
            rs = slice(yy * GRID_W, (yy + 1) * GRID_W)
            o = _attn_core(q_ref[rs, :], k_loc, v_loc, bias, kc, vc, m0)
            y_ref[rs, :] = (o * _silu(g_ref[rs, :])).astype(y_ref.dtype)

    @pl.when(step >= n_lat_steps)
    def _():
        o = _attn_core(q_ref[...], None, None, None, kc, vc, m0)
        y_ref[...] = (o * _silu(g_ref[...])).astype(y_ref.dtype)


def _na_bias_table(rpb, img_rows):
    kh = min(NA_WIN_H, img_rows)
    cols = np.arange(GRID_W)
    col_start = np.clip(cols - NA_WIN_W // 2, 0, GRID_W - NA_WIN_W)
    kx = np.arange(GRID_W)
    inwin = (kx[None, :] >= col_start[:, None]) & (kx[None, :] < col_start[:, None] + NA_WIN_W)
    dx = np.clip(kx[None, :] - cols[:, None] + (NA_WIN_W - 1), 0, 2 * NA_WIN_W - 2)
    delta = np.arange(kh)
    dy = np.arange(kh)[None, :] - delta[:, None] + (NA_WIN_H - 1)
    dy = np.clip(dy, 0, 2 * NA_WIN_H - 2)
    tab = rpb[:, dy[:, None, :, None], dx[None, :, None, :]]
    tab = jnp.where(inwin[None, None, :, None, :], tab, NEG_BIG)
    return tab.reshape(rpb.shape[0], kh, GRID_W, kh * GRID_W)


def _na(proj_b, bias_tab, geom, db):
    m = proj_b.shape[0]
    npair = db // LANES
    img_rows = geom.seq // GRID_W
    rb = geom.ctx
    rows_per_step = rb // GRID_W
    n_lat_steps = geom.seq // rb
    nblk = db // LANES
    def qrow(b, p, s):
        return jnp.where(s < n_lat_steps, b * n_lat_steps + s, geom.n_lat // rb + b)
    kh = min(NA_WIN_H, img_rows)
    return pl.pallas_call(
        functools.partial(_na_kernel, rows_per_step=rows_per_step, img_rows=img_rows, n_lat_steps=n_lat_steps),
        out_shape=jax.ShapeDtypeStruct((m, db), BF16),
        grid=(geom.batch, npair, n_lat_steps + 1),
        in_specs=[
            pl.BlockSpec((rb, LANES), lambda b, p, s: (qrow(b, p, s), 1 * nblk + p)),
            pl.BlockSpec((rb, LANES), lambda b, p, s: (qrow(b, p, s), 4 * nblk + p)),
            pl.BlockSpec((geom.seq, LANES), lambda b, p, s: (b, 2 * nblk + p)),
            pl.BlockSpec((geom.seq, LANES), lambda b, p, s: (b, 3 * nblk + p)),
            pl.BlockSpec((geom.ctx, LANES), lambda b, p, s: (geom.n_lat // geom.ctx + b, 2 * nblk + p)),
            pl.BlockSpec((geom.ctx, LANES), lambda b, p, s: (geom.n_lat // geom.ctx + b, 3 * nblk + p)),
            pl.BlockSpec((2, kh, GRID_W, kh * GRID_W), lambda b, p, s: (p, 0, 0, 0)),
        ],
        out_specs=pl.BlockSpec((rb, LANES), lambda b, p, s: (qrow(b, p, s), p)),
        compiler_params=_cparams(("parallel", "parallel", "arbitrary")),
        name="nbr_attention",
    )(proj_b, proj_b, proj_b, proj_b, proj_b, proj_b, bias_tab)


def _rglru_kernel(*refs, geom, tc, gwc, reverse):
    if reverse:
        (x_ref, hp_ref, hn_ref, cw_ref, cb_ref, wa_ref, ba_ref, wx_ref, bx_ref, lam_ref, hf_ref, g_ref,
         out_ref, carry_ref) = refs
    else:
        (x_ref, hp_ref, hn_ref, cw_ref, cb_ref, wa_ref, ba_ref, wx_ref, bx_ref, lam_ref,
         out_ref, carry_ref) = refs
    b, c = pl.program_id(0), pl.program_id(2)

    @pl.when(c == 0)
    def _():
        carry_ref[...] = jnp.zeros(carry_ref.shape, F32)

    row0 = geom.chunk_block(b, c, tc, reverse) * tc
    is_first, is_last = geom.seq_edges(row0, tc)
    x = x_ref[...]
    nb = _shift_rows(x, hp_ref, hn_ref, slice(None), is_first, is_last, CONV_LEFT, CONV_W - 1 - CONV_LEFT)
    nb[0] = x
    u = cb_ref[...]
    for j in range(CONV_W):
        u = u + cw_ref[j:j + 1, :] * nb[j - CONV_LEFT]
    gate_r = _sigmoid(_bdot(u, wa_ref[0]) + ba_ref[...])
    gate_i = _sigmoid(_bdot(u, wx_ref[0]) + bx_ref[...])
    neg_lam = -lam_ref[...]
    softplus = jnp.maximum(neg_lam, 0.0) + jnp.log1p(jnp.exp(-jnp.abs(neg_lam)))
    log_a = (-RGLRU_C) * gate_r * softplus
    a = jnp.exp(log_a)
    th = jnp.tanh(log_a)
    bv = jnp.sqrt(-2.0 * th / (1.0 - th)) * (gate_i * u)
    ridx = lax.broadcasted_iota(jnp.int32, (tc, 1), 0)
    s = 1
    while s < tc:
        if reverse:
            a_s, b_s = pltpu.roll(a, tc - s, 0), pltpu.roll(bv, tc - s, 0)
            valid = ridx < tc - s
        else:
            a_s, b_s = pltpu.roll(a, s, 0), pltpu.roll(bv, s, 0)
            valid = ridx >= s
        bv = jnp.where(valid, a * b_s + bv, bv)
        a = jnp.where(valid, a * a_s, a)
        s *= 2
    h = bv + a * carry_ref[0:1, :]
    last = 0 if reverse else tc - 1
    carry_ref[...] = jnp.broadcast_to(h[last:last + 1, :], carry_ref.shape)
    if reverse:
        out_ref[...] = ((hf_ref[...] + h) * _silu(g_ref[...])).astype(out_ref.dtype)
    else:
        out_ref[...] = h


def _rglru(proj_c, geom, tc, gwc, conv_w, conv_b, wa_g, ba, wx_g, bx, lam, reverse, h_f=None):
    m = proj_c.shape[0]
    dc = conv_w.shape[1]
    ngrp = dc // gwc
    nch = (geom.seq + geom.ctx) // tc
    nb8 = m // SUBLANES
    r8 = tc // SUBLANES
    blk = lambda b, g, c: geom.chunk_block(b, c, tc, reverse)
    vec = lambda rows: pl.BlockSpec((rows, gwc), lambda b, g, c: (0, g))
    in_specs = [
        pl.BlockSpec((tc, gwc), lambda b, g, c: (blk(b, g, c), g)),
        pl.BlockSpec((SUBLANES, gwc), lambda b, g, c: (jnp.maximum(blk(b, g, c) * r8 - 1, 0), g)),
        pl.BlockSpec((SUBLANES, gwc), lambda b, g, c: (jnp.minimum((blk(b, g, c) + 1) * r8, nb8 - 1), g)),
        vec(CONV_W), vec(1),
        pl.BlockSpec((1, gwc, gwc), lambda b, g, c: (g, 0, 0)), vec(1),
        pl.BlockSpec((1, gwc, gwc), lambda b, g, c: (g, 0, 0)), vec(1),
        vec(1),
    ]
    args = [proj_c, proj_c, proj_c, conv_w, conv_b, wa_g, ba, wx_g, bx, lam]
    if reverse:
        in_specs += [
            pl.BlockSpec((tc, gwc), lambda b, g, c: (blk(b, g, c), g)),
            pl.BlockSpec((tc, gwc), lambda b, g, c: (blk(b, g, c), ngrp + g)),
        ]
        args += [h_f, proj_c]
    return pl.pallas_call(
        functools.partial(_rglru_kernel, geom=geom, tc=tc, gwc=gwc, reverse=reverse),
        out_shape=jax.ShapeDtypeStruct((m, dc), BF16 if reverse else F32),
        grid=(geom.batch, ngrp, nch),
        in_specs=in_specs,
        out_specs=pl.BlockSpec((tc, gwc), lambda b, g, c: (blk(b, g, c), g)),
        scratch_shapes=[pltpu.VMEM((SUBLANES, gwc), F32)],
        compiler_params=_cparams(("parallel", "parallel", "arbitrary")),
        name="rglru_bwd" if reverse else "rglru_fwd",
    )(*args)


def _group_blockdiag(w, gwc):
    nblk, bs, _ = w.shape
    per = gwc // bs
    wg = w.reshape(nblk // per, per, bs, bs)
    eye = jnp.eye(per, dtype=w.dtype)
    dense = jnp.einsum('gpij,pq->gpiqj', wg, eye)
    return dense.reshape(nblk // per, gwc, gwc)


def kernel(x, c, ctx, c_ctx, mod_w, mod_b, norm_pre, norm_post, ev_w_in, ev_mu, ev_w0, ev_w_up, ev_a0, ev_a_up, ev_k_k, ev_k_a, ev_r_k, ev_gn_w, ev_gn_b, ev_rpb, ev_w_out, od_w_in, od_conv_w, od_conv_b, od_gate_a_w, od_gate_a_b, od_gate_x_w, od_gate_x_b, od_lambda, od_w_out):
    batch, seq, d = x.shape
    n_ctx = ctx.shape[1]
    depth = mod_w.shape[0]
    da = ev_k_k.shape[1]
    rw = ev_w_up.shape[2]
    db = ev_w_out.shape[1] - da
    dc = od_conv_w.shape[2]
    rg_bs = od_gate_a_w.shape[-1]
    assert da % LANES == 0 and db == da and 2 * rw == LANES and ev_a_up.shape[2] == rw
    assert batch < SUBLANES and seq % GRID_W == 0 and seq % n_ctx == 0 and n_ctx % CHUNK == 0
    assert n_ctx % GRID_W == 0 and (batch * seq) % n_ctx == 0
    geom = _Geom(batch, seq, n_ctx)
    m, n_lat = geom.m, geom.n_lat

    tm = _pick(math.gcd(n_lat, batch * n_ctx), (512, 256, 128, 64))
    tp = _pick(math.gcd(seq, n_ctx), (256, 128, 64))
    lat_blocks = n_lat // tm
    per_batch = seq // tm
    mod_row = lambda i: jnp.where(i < lat_blocks, i // per_batch, batch)

    c8 = jnp.zeros((SUBLANES, d), F32).at[:batch].set(c).at[batch].set(c_ctx)
    mods = _modulation(c8, mod_w, mod_b).reshape(depth, SUBLANES, 1, 3 * d)

    x_all = jnp.concatenate([x.reshape(n_lat, d), ctx.reshape(batch * n_ctx, d)], axis=0)

    heads_a = da // HEAD
    sel_np = np.zeros((da, LANES), np.float32)
    sel_np[np.arange(da), np.arange(da) // HEAD] = 1.0
    sel = jnp.asarray(sel_np, BF16)
    selt = jnp.asarray(sel_np.T, BF16)
    assert heads_a <= LANES

    a_shifted = 3 * da + 4 * rw
    gwc = rg_bs * LANES // math.gcd(rg_bs, LANES)

    for layer in range(depth):
        last = layer == depth - 1
        i = layer // 2
        mods_l = mods[layer]
        g_pre = norm_pre[layer].reshape(1, d)
        g_post = norm_post[layer].reshape(1, d)
        m_out = n_lat if last else m
        if layer % 2 == 0:
            w_in = ev_w_in[i].astype(BF16)
            proj_a = _inproj(x_all, mods_l, g_pre, w_in[:, :a_shifted], tm, mod_row)
            proj_b = _inproj(x_all, mods_l, g_pre, w_in[:, a_shifted:], tm, mod_row)
            pad_dir = lambda w: jnp.stack([
                jnp.concatenate([w[0], jnp.zeros_like(w[1])], axis=0),
                jnp.concatenate([jnp.zeros_like(w[0]), w[1]], axis=0)]).astype(BF16)
            r, v, kap, kd, bb, lw = _rwkv_prep(
                proj_a, geom, tp, ev_mu[i], ev_w0[i], pad_dir(ev_w_up[i]), ev_a0[i], pad_dir(ev_a_up[i]),
                ev_k_k[i].reshape(1, da), ev_k_a[i].reshape(1, da), sel, selt)
            o_f, o_b = _rwkv_scan(r, v, kap, kd, bb, lw, geom)
            y_a = _rwkv_readout(o_f, o_b, r, v, kd, proj_b, ev_r_k[i].reshape(1, da),
                                ev_gn_w[i].reshape(1, da), ev_gn_b[i].reshape(1, da), sel, selt, tp)
            y_b = _na(proj_b, _na_bias_table(ev_rpb[i], seq // GRID_W), geom, db)
            w_out = ev_w_out[i].astype(BF16)
            x_all = _outproj([y_a, y_b], [w_out[:da], w_out[da:]], x_all, mods_l, g_post, tm, mod_row, m_out)
        else:
            proj_c = _inproj(x_all, mods_l, g_pre, od_w_in[i].astype(BF16), tm, mod_row)
            row = lambda a: a.reshape(1, dc)
            h_f = None
            for dd in range(2):
                out = _rglru(proj_c, geom, tp, gwc, od_conv_w[i], row(od_conv_b[i]),
                             _group_blockdiag(od_gate_a_w[i, dd], gwc).astype(BF16), row(od_gate_a_b[i, dd]),
                             _group_blockdiag(od_gate_x_w[i, dd], gwc).astype(BF16), row(od_gate_x_b[i, dd]),
                             row(od_lambda[i, dd]), dd == 1, h_f)
                h_f = out
            x_all = _outproj([out], [od_w_out[i].astype(BF16)], x_all, mods_l, g_post, tm, mod_row, m_out)
    return x_all.reshape(batch, seq, d)
```

```python
import functools
import math

import jax
import jax.numpy as jnp
import numpy as np
from jax import lax
from jax.experimental import pallas as pl
from jax.experimental.pallas import tpu as pltpu

F32 = jnp.float32
BF16 = jnp.bfloat16

EPS_RMS = 1e-6
GN_EPS = 64e-5
RGLRU_C = 8.0
HEAD = 64
GRID_W = 64
NA_WIN_H = 8
NA_WIN_W = 16
CONV_W = 4
CONV_LEFT = 2
LANES = 128
SUBLANES = 8
CHUNK = 64
NEG_BIG = -1e30
VMEM_LIMIT = 56 * 1024 * 1024


def _cparams(sem):
    return pltpu.CompilerParams(dimension_semantics=sem, vmem_limit_bytes=VMEM_LIMIT)


def _pick(n, cands):
    for c in cands:
        if n % c == 0:
            return c
    raise ValueError(f"no tile for {n} in {cands}")


def _bdot(a, b):
    return jnp.dot(a.astype(BF16), b.astype(BF16), preferred_element_type=F32)


def _bdot_nt(a, b):
    return lax.dot_general(a.astype(BF16), b.astype(BF16), (((1,), (1,)), ((), ())),
                           preferred_element_type=F32)


def _bdot_tn(a, b):
    return lax.dot_general(a.astype(BF16), b.astype(BF16), (((0,), (0,)), ((), ())),
                           preferred_element_type=F32)


def _split3(x):
    hi = x.astype(BF16)
    r1 = x - hi.astype(F32)
    mid = r1.astype(BF16)
    lo = (r1 - mid.astype(F32)).astype(BF16)
    return hi, mid, lo


def _dot_sel_r(x, sel):
    hi, mid, lo = _split3(x)
    d = lambda t: jnp.dot(t, sel, preferred_element_type=F32)
    return d(hi) + d(mid) + d(lo)


def _dot_sel_l(sel, x):
    hi, mid, lo = _split3(x)
    d = lambda t: jnp.dot(sel, t, preferred_element_type=F32)
    return d(hi) + d(mid) + d(lo)


def _sigmoid(x):
    return 1.0 / (1.0 + jnp.exp(-x))


def _silu(x):
    return x * _sigmoid(x)


def _mod_kernel(c_ref, w_ref, b_ref, o_ref):
    cs = _silu(c_ref[...])
    o_ref[...] = _bdot(cs, w_ref[...]) + b_ref[...]


def _modulation(c8, mod_w, mod_b):
    depth, d, n = mod_w.shape
    tn = _pick(n, (1024, 512, 256, 128))
    return pl.pallas_call(
        _mod_kernel,
        out_shape=jax.ShapeDtypeStruct((depth, SUBLANES, n), F32),
        grid=(depth, n // tn),
        in_specs=[
            pl.BlockSpec((SUBLANES, d), lambda l, j: (0, 0)),
            pl.BlockSpec((None, d, tn), lambda l, j: (l, 0, j)),
            pl.BlockSpec((None, 1, tn), lambda l, j: (l, 0, j)),
        ],
        out_specs=pl.BlockSpec((None, SUBLANES, tn), lambda l, j: (l, 0, j)),
        compiler_params=_cparams(("parallel", "parallel")),
        name="modulation",
    )(c8, mod_w, mod_b.reshape(depth, 1, n))


def _inproj_kernel(x_ref, mod_ref, g_ref, w_ref, o_ref, h_ref, *, d):
    @pl.when(pl.program_id(1) == 0)
    def _():
        x = x_ref[...]
        ms = jnp.mean(x * x, axis=-1, keepdims=True)
        y = x * lax.rsqrt(ms + EPS_RMS) * g_ref[...]
        h_ref[...] = (y * (1.0 + mod_ref[:, d:2 * d]) + mod_ref[:, 0:d]).astype(BF16)

    o_ref[...] = jnp.dot(h_ref[...], w_ref[...], preferred_element_type=F32)


def _inproj(x_all, mods_l, g, w_bf, tm, mod_row):
    m, d = x_all.shape
    n = w_bf.shape[1]
    tn = _pick(n, (1664, 1408, 1280, 1024, 768, 640, 512, 384, 256, 128))
    return pl.pallas_call(
        functools.partial(_inproj_kernel, d=d),
        out_shape=jax.ShapeDtypeStruct((m, n), F32),
        grid=(m // tm, n // tn),
        in_specs=[
            pl.BlockSpec((tm, d), lambda i, j: (i, 0)),
            pl.BlockSpec((None, 1, 3 * d), lambda i, j: (mod_row(i), 0, 0)),
            pl.BlockSpec((1, d), lambda i, j: (0, 0)),
            pl.BlockSpec((d, tn), lambda i, j: (0, j)),
        ],
        out_specs=pl.BlockSpec((tm, tn), lambda i, j: (i, j)),
        scratch_shapes=[pltpu.VMEM((tm, d), BF16)],
        compiler_params=_cparams(("parallel", "arbitrary")),
        name="inproj",
    )(x_all, mods_l, g, w_bf)


def _outproj_kernel(*refs, n_in, d):
    y_refs = refs[:n_in]
    w_refs = refs[n_in:2 * n_in]
    x_ref, mod_ref, g_ref, o_ref = refs[2 * n_in:]
    acc = jnp.dot(y_refs[0][...], w_refs[0][...], preferred_element_type=F32)
    for i in range(1, n_in):
        acc = acc + jnp.dot(y_refs[i][...], w_refs[i][...], preferred_element_type=F32)
    ms = jnp.mean(acc * acc, axis=-1, keepdims=True)
    z = acc * lax.rsqrt(ms + EPS_RMS) * g_ref[...]
    o_ref[...] = x_ref[...] + mod_ref[:, 2 * d:3 * d] * z


def _outproj(ys, ws, x_all, mods_l, g, tm, mod_row, m_out):
    d = x_all.shape[1]
    n_in = len(ys)
    in_specs = [pl.BlockSpec((tm, y.shape[1]), lambda i: (i, 0)) for y in ys]
    in_specs += [pl.BlockSpec(w.shape, lambda i: (0, 0)) for w in ws]
    in_specs += [
        pl.BlockSpec((tm, d), lambda i: (i, 0)),
        pl.BlockSpec((None, 1, 3 * d), lambda i: (mod_row(i), 0, 0)),
        pl.BlockSpec((1, d), lambda i: (0, 0)),
    ]
    return pl.pallas_call(
        functools.partial(_outproj_kernel, n_in=n_in, d=d),
        out_shape=jax.ShapeDtypeStruct((m_out, d), F32),
        grid=(m_out // tm,),
        in_specs=in_specs,
        out_specs=pl.BlockSpec((tm, d), lambda i: (i, 0)),
        compiler_params=_cparams(("parallel",)),
        name="outproj",
    )(*ys, *ws, x_all, mods_l, g)


class _Geom:
    def __init__(self, batch, seq, ctx):
        self.batch, self.seq, self.ctx = batch, seq, ctx
        self.n_lat = batch * seq
        self.m = batch * (seq + ctx)

    def seq_edges(self, row0, rows):
        in_lat = row0 < self.n_lat
        off = jnp.where(in_lat, row0 % self.seq, (row0 - self.n_lat) % self.ctx)
        length = jnp.where(in_lat, self.seq, self.ctx)
        return off == 0, off + rows == length

    def chunk_block(self, b, c, rows, reverse):
        cc = self.ctx // rows
        sc = self.seq // rows
        if reverse:
            ctx_blk = self.n_lat // rows + b * cc + (cc - 1 - c)
            lat_blk = b * sc + (sc - 1 - (c - cc))
        else:
            ctx_blk = self.n_lat // rows + b * cc + c
            lat_blk = b * sc + (c - cc)
        return jnp.where(c < cc, ctx_blk, lat_blk)


def _shift_rows(x, hp_ref, hn_ref, cols, is_first, is_last, back, fwd):
    rows = x.shape[0]
    ridx = lax.broadcasted_iota(jnp.int32, (rows, 1), 0)
    outs = {}
    for s in range(1, back + 1):
        y = pltpu.roll(x, s, 0)
        for j in range(s):
            halo = hp_ref[SUBLANES - s + j:SUBLANES - s + j + 1, cols]
            halo = jnp.where(is_first, 0.0, halo)
            y = jnp.where(ridx == j, halo, y)
        outs[-s] = y
    for s in range(1, fwd + 1):
        y = pltpu.roll(x, rows - s, 0)
        for j in range(s):
            halo = hn_ref[j:j + 1, cols]
            halo = jnp.where(is_last, 0.0, halo)
            y = jnp.where(ridx == rows - s + j, halo, y)
        outs[s] = y
    return outs


def _halo_specs(tp, width, m):
    nb8 = m // SUBLANES
    r8 = tp // SUBLANES
    return [
        pl.BlockSpec((SUBLANES, width), lambda i: (jnp.maximum(i * r8 - 1, 0), 0)),
        pl.BlockSpec((SUBLANES, width), lambda i: (jnp.minimum((i + 1) * r8, nb8 - 1), 0)),
    ]


def _rwkv_prep_kernel(f_ref, hp_ref, hn_ref, mu_ref, w0_ref, wup_ref, a0_ref, aup_ref, kk_ref, ka_ref,
                      sel_ref, selt_ref, r_ref, v_ref, kap_ref, kd_ref, bb_ref, lw_ref, *, geom, tp, da):
    i = pl.program_id(0)
    is_first, is_last = geom.seq_edges(i * tp, tp)

    def shifted(lo, hi):
        cols = slice(lo, hi)
        x = f_ref[:, cols]
        nb = _shift_rows(x, hp_ref, hn_ref, cols, is_first, is_last, 1, 1)
        return x + mu_ref[0:1, cols] * (nb[-1] - x) + mu_ref[1:2, cols] * (nb[1] - x)

    r = shifted(0, da)
    k = shifted(da, 2 * da)
    v = shifted(2 * da, 3 * da)
    cw = jnp.tanh(shifted(3 * da, 3 * da + LANES))
    ca = shifted(3 * da + LANES, 3 * da + 2 * LANES)
    r_ref[...] = r
    v_ref[...] = v
    kk = k * kk_ref[...]
    ss = _dot_sel_r(kk * kk, sel_ref[...])
    inv = lax.rsqrt(ss + 1e-12)
    kap = kk * _dot_sel_r(inv, selt_ref[...])
    kap_ref[...] = kap
    for dd in range(2):
        zw = w0_ref[dd:dd + 1, :] + _bdot(cw, wup_ref[dd])
        lw_ref[dd] = -math.exp(-0.5) * _sigmoid(zw)
        a = _sigmoid(a0_ref[dd:dd + 1, :] + _bdot(ca, aup_ref[dd]))
        kd_ref[dd] = k * (1.0 + (a - 1.0) * ka_ref[...])
        bb_ref[dd] = kap * a


def _rwkv_prep(proj_a, geom, tp, mu, w0, wup_pad, a0, aup_pad, k_k, k_a, sel, selt):
    m, wa = proj_a.shape
    da = k_k.shape[1]
    full = lambda a: pl.BlockSpec(a.shape, lambda i: (0,) * a.ndim)
    tok = jax.ShapeDtypeStruct((m, da), F32)
    tok2 = jax.ShapeDtypeStruct((2, m, da), F32)
    params = (mu, w0, wup_pad, a0, aup_pad, k_k, k_a, sel, selt)
    return pl.pallas_call(
        functools.partial(_rwkv_prep_kernel, geom=geom, tp=tp, da=da),
        out_shape=(tok, tok, tok, tok2, tok2, tok2),
        grid=(m // tp,),
        in_specs=[pl.BlockSpec((tp, wa), lambda i: (i, 0))] + _halo_specs(tp, wa, m) + [full(p) for p in params],
        out_specs=(
            pl.BlockSpec((tp, da), lambda i: (i, 0)),
            pl.BlockSpec((tp, da), lambda i: (i, 0)),
            pl.BlockSpec((tp, da), lambda i: (i, 0)),
            pl.BlockSpec((2, tp, da), lambda i: (0, i, 0)),
            pl.BlockSpec((2, tp, da), lambda i: (0, i, 0)),
            pl.BlockSpec((2, tp, da), lambda i: (0, i, 0)),
        ),
        compiler_params=_cparams(("parallel",)),
        name="rwkv_prep",
    )(proj_a, proj_a, proj_a, *params)


def _rwkv_chunks(jobs):
    L = CHUNK
    each = lambda f, *cols: [f(*a) for a in zip(*cols)]
    s_prev, r, v, kap, kd, bb, lw, consts = (list(c) for c in zip(*jobs))
    tri_sel, bd_strict, row_incl, bd_full, m0, m1, eye, last_row, levels = (list(c) for c in zip(*consts))
    stack = lambda x: jnp.concatenate([x * m0[0], x * m1[0]], axis=0)

    cum = each(_dot_sel_l, tri_sel, lw)
    tot = each(lambda c, lr: c[lr:lr + 1, :], cum, last_row)
    rt = each(lambda x, c: x * jnp.exp(c), r, cum)
    e_neg = each(lambda c: jnp.exp(-c), cum)
    kt = each(lambda x, e: x * e, kd, e_neg)
    bt = each(lambda x, e: x * e, bb, e_neg)
    kap_s = each(lambda x, c, l: stack(x * jnp.exp(c - l)), kap, cum, lw)
    vv_s = each(stack, v)
    g = each(lambda ks, x, b_, k_: _bdot_nt(jnp.concatenate([ks, x], axis=0),
                                            jnp.concatenate([stack(b_), stack(k_)], axis=0)),
             kap_s, rt, bt, kt)
    n = each(lambda m_, x: jnp.where(m_, -x[:2 * L, :2 * L], 0.0), bd_strict, g)
    mk = each(lambda m_, x: jnp.where(m_, x[:2 * L, 2 * L:], 0.0), bd_strict, g)
    arb = each(lambda m_, x: jnp.where(m_, x[2 * L:, :2 * L], 0.0), row_incl, g)
    ark = each(lambda m_, x: jnp.where(m_, x[2 * L:, 2 * L:], 0.0), row_incl, g)
    tm = each(lambda e, lv, x: e + jnp.where(lv[0], x, 0.0), eye, levels, n)
    for li in range(1, len(levels[0])):
        x = each(lambda lv, n_, t: _bdot(jnp.where(lv[li], n_, 0.0), t), levels, n, tm)
        tm = each(lambda t, x_: t + _bdot(t, x_), tm, x)
    mkv = each(_bdot, mk, vv_s)
    wy = each(lambda t, ks, x: _bdot(t, jnp.concatenate([ks, x], axis=1)), tm, kap_s, mkv)
    w = each(lambda x: x[:L, :LANES] + x[L:, :LANES], wy)
    y = each(lambda x: x[:L, LANES:] + x[L:, LANES:], wy)
    z = each(_bdot, ark, vv_s)
    rest = each(lambda t, c: jnp.exp(t - c), tot, cum)
    gk = each(lambda v_, k_, e: _bdot_tn(v_, k_ * e), v, kd, rest)
    wr = each(lambda w_, x, s: _bdot_nt(jnp.concatenate([w_, x], axis=0), s), w, rt, s_prev)
    u = each(lambda x, y_: -(x[:L] + y_), wr, y)
    o = each(lambda x, a, u_, z_: x[L:] + _bdot(a, stack(u_)) + z_, wr, arb, u, z)
    s_new = each(lambda m_, s, t, u_, b_, e, g_: jnp.where(m_, s * jnp.exp(t) + _bdot_tn(u_, b_ * e) + g_, 0.0),
                 bd_full, s_prev, tot, u, bb, rest, gk)
    return list(zip(s_new, o))


def _rwkv_consts(reverse):
    L = CHUNK
    ri = lax.broadcasted_iota(jnp.int32, (2 * L, 2 * L), 0)
    ci = lax.broadcasted_iota(jnp.int32, (2 * L, 2 * L), 1)
    same = (ri >= L) == (ci >= L)
    rt, ct = ri & (L - 1), ci & (L - 1)
    bd_strict = same & ((ct > rt) if reverse else (ct < rt))
    r1 = lax.broadcasted_iota(jnp.int32, (L, 2 * L), 0)
    c1 = lax.broadcasted_iota(jnp.int32, (L, 2 * L), 1) & (L - 1)
    row_incl = (c1 >= r1) if reverse else (c1 <= r1)
    r2 = lax.broadcasted_iota(jnp.int32, (L, L), 0)
    c2 = lax.broadcasted_iota(jnp.int32, (L, L), 1)
    tri_sel = jnp.where((c2 >= r2) if reverse else (c2 <= r2), 1.0, 0.0).astype(BF16)
    lane = lax.broadcasted_iota(jnp.int32, (1, LANES), 1)
    m0 = jnp.where(lane < HEAD, 1.0, 0.0)
    m1 = 1.0 - m0
    eye = jnp.where(ri == ci, 1.0, 0.0)
    levels = tuple(((ri >> (l + 1)) == (ci >> (l + 1))) & ((ri >> l) != (ci >> l))
                   for l in range(int(math.log2(L))))
    return (tri_sel, bd_strict, row_incl, same, m0, m1, eye, 0 if reverse else L - 1, levels)


def _rwkv_scan_kernel(*refs, npairs):
    ins = refs[:12]
    of_ref, ob_ref, s_ref = refs[12:]

    @pl.when(pl.program_id(2) == 0)
    def _():
        s_ref[...] = jnp.zeros(s_ref.shape, F32)

    jobs, dests = [], []
    for dd, o_ref in ((0, of_ref), (1, ob_ref)):
        consts = _rwkv_consts(dd == 1)
        r_ref, v_ref, kap_ref, kd_ref, bb_ref, lw_ref = ins[6 * dd:6 * dd + 6]
        for p in range(npairs):
            sl = slice(p * LANES, (p + 1) * LANES)
            jobs.append((s_ref[dd, p], r_ref[:, sl], v_ref[:, sl], kap_ref[:, sl],
                         kd_ref[:, sl], bb_ref[:, sl], lw_ref[:, sl], consts))
            dests.append((dd, p, o_ref, sl))
    for (s_new, o), (dd, p, o_ref, sl) in zip(_rwkv_chunks(jobs), dests):
        s_ref[dd, p] = s_new
        o_ref[:, sl] = o


def _rwkv_scan(r, v, kap, kd, bb, lw, geom):
    m, da = r.shape
    gw = _pick(da, (512, 256, 128))
    npairs = gw // LANES
    nch = (geom.seq + geom.ctx) // CHUNK
    in_specs = []
    for dd in range(2):
        blk = lambda b, g, c, dd=dd: geom.chunk_block(b, c, CHUNK, dd == 1)
        tok = pl.BlockSpec((CHUNK, gw), lambda b, g, c, blk=blk: (blk(b, g, c), g))
        tokd = pl.BlockSpec((None, CHUNK, gw), lambda b, g, c, blk=blk, dd=dd: (dd, blk(b, g, c), g))
        in_specs += [tok, tok, tok, tokd, tokd, tokd]
    out_specs = tuple(
        pl.BlockSpec((CHUNK, gw), lambda b, g, c, dd=dd: (geom.chunk_block(b, c, CHUNK, dd == 1), g))
        for dd in range(2))
    tok_shape = jax.ShapeDtypeStruct((m, da), F32)
    return pl.pallas_call(
        functools.partial(_rwkv_scan_kernel, npairs=npairs),
        out_shape=(tok_shape, tok_shape),
        grid=(geom.batch, da // gw, nch),
        in_specs=in_specs,
        out_specs=out_specs,
        scratch_shapes=[pltpu.VMEM((2, npairs, LANES, LANES), F32)],
        compiler_params=_cparams(("parallel", "parallel", "arbitrary")),
        name="rwkv_scan",
    )(r, v, kap, kd, bb, lw, r, v, kap, kd, bb, lw)


def _rwkv_readout_kernel(of_ref, ob_ref, r_ref, v_ref, kd_ref, g_ref, rk_ref, gw_ref, gb_ref,
                         sel_ref, selt_ref, y_ref):
    sel, selt = sel_ref[...], selt_ref[...]
    head_mean = lambda x: _dot_sel_r(_dot_sel_r(x, sel) * (1.0 / HEAD), selt)
    o = of_ref[...] + ob_ref[...]
    cen = o - head_mean(o)
    var = head_mean(cen * cen)
    on = cen * lax.rsqrt(var + GN_EPS) * gw_ref[...] + gb_ref[...]
    bonus = _dot_sel_r(_dot_sel_r(r_ref[...] * (kd_ref[0] + kd_ref[1]) * rk_ref[...], sel), selt)
    y_ref[...] = ((on + bonus * v_ref[...]) * _silu(g_ref[...])).astype(y_ref.dtype)


def _rwkv_readout(o_f, o_b, r, v, kd, proj_b, r_k, gn_w, gn_b, sel, selt, tr):
    m, da = r.shape
    full = lambda a: pl.BlockSpec(a.shape, lambda i: (0,) * a.ndim)
    tok = pl.BlockSpec((tr, da), lambda i: (i, 0))
    params = (r_k, gn_w, gn_b, sel, selt)
    return pl.pallas_call(
        _rwkv_readout_kernel,
        out_shape=jax.ShapeDtypeStruct((m, da), BF16),
        grid=(m // tr,),
        in_specs=[tok, tok, tok, tok, pl.BlockSpec((2, tr, da), lambda i: (0, i, 0)), tok] + [full(p) for p in params],
        out_specs=tok,
        compiler_params=_cparams(("parallel",)),
        name="rwkv_readout",
    )(o_f, o_b, r, v, kd, proj_b, *params)


def _attn_core(q, k_loc, v_loc, bias, kc, vc, m0):
    n = q.shape[0]
    qs = jnp.concatenate([q * m0, q * (1.0 - m0)], axis=0) * (HEAD ** -0.5)
    s_ctx = _bdot_nt(qs, kc)
    mx = jnp.max(s_ctx, axis=-1, keepdims=True)
    if k_loc is not None:
        s_loc = _bdot_nt(qs, k_loc) + bias
        mx = jnp.maximum(mx, jnp.max(s_loc, axis=-1, keepdims=True))
        p_loc = jnp.exp(s_loc - mx)
    p_ctx = jnp.exp(s_ctx - mx)
    den = jnp.sum(p_ctx, axis=-1, keepdims=True)
    acc = _bdot(p_ctx, vc)
    if k_loc is not None:
        den = den + jnp.sum(p_loc, axis=-1, keepdims=True)
        acc = acc + _bdot(p_loc, v_loc)
    acc = acc / den
    return jnp.where(m0 > 0.5, acc[:n], acc[n:])


def _na_kernel(q_ref, g_ref, k_ref, v_ref, kc_ref, vc_ref, bias_ref, y_ref, *, rows_per_step, img_rows, n_lat_steps):
    step = pl.program_id(2)
    lane = lax.broadcasted_iota(jnp.int32, (1, LANES), 1)
    m0 = jnp.where(lane < HEAD, 1.0, 0.0)
    kc = kc_ref[...]
    vc = vc_ref[...]
    kh = min(NA_WIN_H, img_rows)

    @pl.when(step < n_lat_steps)
    def _():
        for yy in range(rows_per_step):
            y = step * rows_per_step + yy
            y0 = jnp.clip(y - kh // 2, 0, img_rows - kh)
            start = pl.multiple_of(y0 * GRID_W, GRID_W)
            k_loc = k_ref[pl.ds(start, kh * GRID_W), :]
            v_loc = v_ref[pl.ds(start, kh * GRID_W), :]
            bias = bias_ref[:, y - y0].reshape(2 * GRID_W, kh * GRID_W)
            rs = slice(yy * GRID_W, (yy + 1) * GRID_W)
            o = _attn_core(q_ref[rs, :], k_loc, v_loc, bias, kc, vc, m0)
            y_ref[rs, :] = (o * _silu(g_ref[rs, :])).astype(y_ref.dtype)

    @pl.when(step >= n_lat_steps)
    def _():
        o = _attn_core(q_ref[...], None, None, None, kc, vc, m0)
        y_ref[...] = (o * _silu(g_ref[...])).astype(y_ref.dtype)


def _na_bias_table(rpb, img_rows):
    kh = min(NA_WIN_H, img_rows)
    cols = np.arange(GRID_W)
    col_start = np.clip(cols - NA_WIN_W // 2, 0, GRID_W - NA_WIN_W)
    kx = np.arange(GRID_W)
    inwin = (kx[None, :] >= col_start[:, None]) & (kx[None, :] < col_start[:, None] + NA_WIN_W)
    dx = kx[None, :] - cols[:, None] + (NA_WIN_W - 1)
    onehot = (np.arange(2 * NA_WIN_W - 1)[:, None, None] == dx[None]) & inwin[None]
    toep = jnp.einsum('hyd,dxk->hyxk', rpb, jnp.asarray(onehot, rpb.dtype), precision=lax.Precision.HIGHEST)
    toep = jnp.where(inwin[None, None], toep, NEG_BIG)
    tabs = [toep[:, NA_WIN_H - 1 - dl:NA_WIN_H - 1 - dl + kh] for dl in range(kh)]
    tab = jnp.stack(tabs, axis=1).transpose(0, 1, 3, 2, 4)
    return tab.reshape(rpb.shape[0], kh, GRID_W, kh * GRID_W)


def _na(proj_b, bias_tab, geom, db):
    m = proj_b.shape[0]
    npair = db // LANES
    img_rows = geom.seq // GRID_W
    rb = geom.ctx
    rows_per_step = rb // GRID_W
    n_lat_steps = geom.seq // rb
    nblk = db // LANES
    def qrow(b, p, s):
        return jnp.where(s < n_lat_steps, b * n_lat_steps + s, geom.n_lat // rb + b)
    kh = min(NA_WIN_H, img_rows)
    return pl.pallas_call(
        functools.partial(_na_kernel, rows_per_step=rows_per_step, img_rows=img_rows, n_lat_steps=n_lat_steps),
        out_shape=jax.ShapeDtypeStruct((m, db), BF16),
        grid=(geom.batch, npair, n_lat_steps + 1),
        in_specs=[
            pl.BlockSpec((rb, LANES), lambda b, p, s: (qrow(b, p, s), 1 * nblk + p)),
            pl.BlockSpec((rb, LANES), lambda b, p, s: (qrow(b, p, s), 4 * nblk + p)),
            pl.BlockSpec((geom.seq, LANES), lambda b, p, s: (b, 2 * nblk + p)),
            pl.BlockSpec((geom.seq, LANES), lambda b, p, s: (b, 3 * nblk + p)),
            pl.BlockSpec((geom.ctx, LANES), lambda b, p, s: (geom.n_lat // geom.ctx + b, 2 * nblk + p)),
            pl.BlockSpec((geom.ctx, LANES), lambda b, p, s: (geom.n_lat // geom.ctx + b, 3 * nblk + p)),
            pl.BlockSpec((2, kh, GRID_W, kh * GRID_W), lambda b, p, s: (p, 0, 0, 0)),
        ],
        out_specs=pl.BlockSpec((rb, LANES), lambda b, p, s: (qrow(b, p, s), p)),
        compiler_params=_cparams(("parallel", "parallel", "arbitrary")),
        name="nbr_attention",
    )(proj_b, proj_b, proj_b, proj_b, proj_b, proj_b, bias_tab)


def _rglru_kernel(*refs, geom, tc, gwc, reverse):
    if reverse:
        (x_ref, hp_ref, hn_ref, cw_ref, cb_ref, wa_ref, ba_ref, wx_ref, bx_ref, lam_ref, hf_ref, g_ref,
         out_ref, carry_ref) = refs
    else:
        (x_ref, hp_ref, hn_ref, cw_ref, cb_ref, wa_ref, ba_ref, wx_ref, bx_ref, lam_ref,
         out_ref, carry_ref) = refs
    b, c = pl.program_id(0), pl.program_id(2)

    @pl.when(c == 0)
    def _():
        carry_ref[...] = jnp.zeros(carry_ref.shape, F32)

    row0 = geom.chunk_block(b, c, tc, reverse) * tc
    is_first, is_last = geom.seq_edges(row0, tc)
    x = x_ref[...]
    nb = _shift_rows(x, hp_ref, hn_ref, slice(None), is_first, is_last, CONV_LEFT, CONV_W - 1 - CONV_LEFT)
    nb[0] = x
    u = cb_ref[...]
    for j in range(CONV_W):
        u = u + cw_ref[j:j + 1, :] * nb[j - CONV_LEFT]
    gate_r = _sigmoid(_bdot(u, wa_ref[0]) + ba_ref[...])
    gate_i = _sigmoid(_bdot(u, wx_ref[0]) + bx_ref[...])
    neg_lam = -lam_ref[...]
    softplus = jnp.maximum(neg_lam, 0.0) + jnp.log1p(jnp.exp(-jnp.abs(neg_lam)))
    log_a = (-RGLRU_C) * gate_r * softplus
    a = jnp.exp(log_a)
    th = jnp.tanh(log_a)
    bv = jnp.sqrt(-2.0 * th / (1.0 - th)) * (gate_i * u)
    ridx = lax.broadcasted_iota(jnp.int32, (tc, 1), 0)
    s = 1
    while s < tc:
        if reverse:
            a_s, b_s = pltpu.roll(a, tc - s, 0), pltpu.roll(bv, tc - s, 0)
            valid = ridx < tc - s
        else:
            a_s, b_s = pltpu.roll(a, s, 0), pltpu.roll(bv, s, 0)
            valid = ridx >= s
        bv = jnp.where(valid, a * b_s + bv, bv)
        a = jnp.where(valid, a * a_s, a)
        s *= 2
    h = bv + a * carry_ref[0:1, :]
    last = 0 if reverse else tc - 1
    carry_ref[...] = jnp.broadcast_to(h[last:last + 1, :], carry_ref.shape)
    if reverse:
        out_ref[...] = ((hf_ref[...] + h) * _silu(g_ref[...])).astype(out_ref.dtype)
    else:
        out_ref[...] = h


def _rglru(proj_c, geom, tc, gwc, conv_w, conv_b, wa_g, ba, wx_g, bx, lam, reverse, h_f=None):
    m = proj_c.shape[0]
    dc = conv_w.shape[1]
    ngrp = dc // gwc
    nch = (geom.seq + geom.ctx) // tc
    nb8 = m // SUBLANES
    r8 = tc // SUBLANES
    blk = lambda b, g, c: geom.chunk_block(b, c, tc, reverse)
    vec = lambda rows: pl.BlockSpec((rows, gwc), lambda b, g, c: (0, g))
    in_specs = [
        pl.BlockSpec((tc, gwc), lambda b, g, c: (blk(b, g, c), g)),
        pl.BlockSpec((SUBLANES, gwc), lambda b, g, c: (jnp.maximum(blk(b, g, c) * r8 - 1, 0), g)),
        pl.BlockSpec((SUBLANES, gwc), lambda b, g, c: (jnp.minimum((blk(b, g, c) + 1) * r8, nb8 - 1), g)),
        vec(CONV_W), vec(1),
        pl.BlockSpec((1, gwc, gwc), lambda b, g, c: (g, 0, 0)), vec(1),
        pl.BlockSpec((1, gwc, gwc), lambda b, g, c: (g, 0, 0)), vec(1),
        vec(1),
    ]
    args = [proj_c, proj_c, proj_c, conv_w, conv_b, wa_g, ba, wx_g, bx, lam]
    if reverse:
        in_specs += [
            pl.BlockSpec((tc, gwc), lambda b, g, c: (blk(b, g, c), g)),
            pl.BlockSpec((tc, gwc), lambda b, g, c: (blk(b, g, c), ngrp + g)),
        ]
        args += [h_f, proj_c]
    return pl.pallas_call(
        functools.partial(_rglru_kernel, geom=geom, tc=tc, gwc=gwc, reverse=reverse),
        out_shape=jax.ShapeDtypeStruct((m, dc), BF16 if reverse else F32),
        grid=(geom.batch, ngrp, nch),
        in_specs=in_specs,
        out_specs=pl.BlockSpec((tc, gwc), lambda b, g, c: (blk(b, g, c), g)),
        scratch_shapes=[pltpu.VMEM((SUBLANES, gwc), F32)],
        compiler_params=_cparams(("parallel", "parallel", "arbitrary")),
        name="rglru_bwd" if reverse else "rglru_fwd",
    )(*args)


def _group_blockdiag(w, gwc):
    nblk, bs, _ = w.shape
    per = gwc // bs
    wg = w.reshape(nblk // per, per, bs, bs)
    eye = jnp.eye(per, dtype=w.dtype)
    dense = jnp.einsum('gpij,pq->gpiqj', wg, eye)
    return dense.reshape(nblk // per, gwc, gwc)


def kernel(x, c, ctx, c_ctx, mod_w, mod_b, norm_pre, norm_post, ev_w_in, ev_mu, ev_w0, ev_w_up, ev_a0, ev_a_up, ev_k_k, ev_k_a, ev_r_k, ev_gn_w, ev_gn_b, ev_rpb, ev_w_out, od_w_in, od_conv_w, od_conv_b, od_gate_a_w, od_gate_a_b, od_gate_x_w, od_gate_x_b, od_lambda, od_w_out):
    batch, seq, d = x.shape
    n_ctx = ctx.shape[1]
    depth = mod_w.shape[0]
    da = ev_k_k.shape[1]
    rw = ev_w_up.shape[2]
    db = ev_w_out.shape[1] - da
    dc = od_conv_w.shape[2]
    rg_bs = od_gate_a_w.shape[-1]
    assert da % LANES == 0 and db == da and 2 * rw == LANES and ev_a_up.shape[2] == rw
    assert batch < SUBLANES and seq % GRID_W == 0 and seq % n_ctx == 0 and n_ctx % CHUNK == 0
    assert n_ctx % GRID_W == 0 and (batch * seq) % n_ctx == 0
    geom = _Geom(batch, seq, n_ctx)
    m, n_lat = geom.m, geom.n_lat

    tm = _pick(math.gcd(n_lat, batch * n_ctx), (512, 256, 128, 64))
    tp = _pick(math.gcd(seq, n_ctx), (256, 128, 64))
    lat_blocks = n_lat // tm
    per_batch = seq // tm
    mod_row = lambda i: jnp.where(i < lat_blocks, i // per_batch, batch)

    c8 = jnp.zeros((SUBLANES, d), F32).at[:batch].set(c).at[batch].set(c_ctx)
    mods = _modulation(c8, mod_w, mod_b).reshape(depth, SUBLANES, 1, 3 * d)

    x_all = jnp.concatenate([x.reshape(n_lat, d), ctx.reshape(batch * n_ctx, d)], axis=0)

    heads_a = da // HEAD
    sel_np = np.zeros((da, LANES), np.float32)
    sel_np[np.arange(da), np.arange(da) // HEAD] = 1.0
    sel = jnp.asarray(sel_np, BF16)
    selt = jnp.asarray(sel_np.T, BF16)
    assert heads_a <= LANES

    a_shifted = 3 * da + 4 * rw
    gwc = rg_bs * LANES // math.gcd(rg_bs, LANES)

    for layer in range(depth):
        last = layer == depth - 1
        i = layer // 2
        mods_l = mods[layer]
        g_pre = norm_pre[layer].reshape(1, d)
        g_post = norm_post[layer].reshape(1, d)
        m_out = n_lat if last else m
        if layer % 2 == 0:
            w_in = ev_w_in[i].astype(BF16)
            proj_a = _inproj(x_all, mods_l, g_pre, w_in[:, :a_shifted], tm, mod_row)
            proj_b = _inproj(x_all, mods_l, g_pre, w_in[:, a_shifted:], tm, mod_row)
            pad_dir = lambda w: jnp.stack([
                jnp.concatenate([w[0], jnp.zeros_like(w[1])], axis=0),
                jnp.concatenate([jnp.zeros_like(w[0]), w[1]], axis=0)]).astype(BF16)
            r, v, kap, kd, bb, lw = _rwkv_prep(
                proj_a, geom, tp, ev_mu[i], ev_w0[i], pad_dir(ev_w_up[i]), ev_a0[i], pad_dir(ev_a_up[i]),
                ev_k_k[i].reshape(1, da), ev_k_a[i].reshape(1, da), sel, selt)
            o_f, o_b = _rwkv_scan(r, v, kap, kd, bb, lw, geom)
            y_a = _rwkv_readout(o_f, o_b, r, v, kd, proj_b, ev_r_k[i].reshape(1, da),
                                ev_gn_w[i].reshape(1, da), ev_gn_b[i].reshape(1, da), sel, selt, tp)
            y_b = _na(proj_b, _na_bias_table(ev_rpb[i], seq // GRID_W), geom, db)
            w_out = ev_w_out[i].astype(BF16)
            x_all = _outproj([y_a, y_b], [w_out[:da], w_out[da:]], x_all, mods_l, g_post, tm, mod_row, m_out)
        else:
            proj_c = _inproj(x_all, mods_l, g_pre, od_w_in[i].astype(BF16), tm, mod_row)
            row = lambda a: a.reshape(1, dc)
            h_f = None
            for dd in range(2):
                out = _rglru(proj_c, geom, tp, gwc, od_conv_w[i], row(od_conv_b[i]),
                             _group_blockdiag(od_gate_a_w[i, dd], gwc).astype(BF16), row(od_gate_a_b[i, dd]),
                             _group_blockdiag(od_gate_x_w[i, dd], gwc).astype(BF16), row(od_gate_x_b[i, dd]),
                             row(od_lambda[i, dd]), dd == 1, h_f)
                h_f = out
            x_all = _outproj([out], [od_w_out[i].astype(BF16)], x_all, mods_l, g_post, tm, mod_row, m_out)
    return x_all.reshape(batch, seq, d)
```

```python
import functools
import math

import jax
import jax.numpy as jnp
import numpy as np
from jax import lax
from jax.experimental import pallas as pl
from jax.experimental.pallas import tpu as pltpu

F32 = jnp.float32
BF16 = jnp.bfloat16

EPS_RMS = 1e-6
GN_EPS = 64e-5
RGLRU_C = 8.0
HEAD = 64
GRID_W = 64
NA_WIN_H = 8
NA_WIN_W = 16
CONV_W = 4
CONV_LEFT = 2
LANES = 128
SUBLANES = 8
CHUNK = 64
NEG_BIG = -1e30
VMEM_LIMIT = 56 * 1024 * 1024


def _cparams(sem):
    return pltpu.CompilerParams(dimension_semantics=sem, vmem_limit_bytes=VMEM_LIMIT)


def _pick(n, cands):
    for c in cands:
        if n % c == 0:
            return c
    raise ValueError(f"no tile for {n} in {cands}")


def _bdot(a, b):
    return jnp.dot(a.astype(BF16), b.astype(BF16), preferred_element_type=F32)


def _bdot_nt(a, b):
    return lax.dot_general(a.astype(BF16), b.astype(BF16), (((1,), (1,)), ((), ())),
                           preferred_element_type=F32)


def _bdot_tn(a, b):
    return lax.dot_general(a.astype(BF16), b.astype(BF16), (((0,), (0,)), ((), ())),
                           preferred_element_type=F32)


def _split3(x):
    hi = x.astype(BF16)
    r1 = x - hi.astype(F32)
    mid = r1.astype(BF16)
    lo = (r1 - mid.astype(F32)).astype(BF16)
    return hi, mid, lo


def _dot_sel_r(x, sel):
    hi, mid, lo = _split3(x)
    d = lambda t: jnp.dot(t, sel, preferred_element_type=F32)
    return d(hi) + d(mid) + d(lo)


def _dot_sel_l(sel, x):
    hi, mid, lo = _split3(x)
    d = lambda t: jnp.dot(sel, t, preferred_element_type=F32)
    return d(hi) + d(mid) + d(lo)


def _sigmoid(x):
    return 1.0 / (1.0 + jnp.exp(-x))


def _silu(x):
    return x * _sigmoid(x)


def _mod_kernel(c_ref, w_ref, b_ref, o_ref):
    cs = _silu(c_ref[...])
    o_ref[...] = _bdot(cs, w_ref[...]) + b_ref[...]


def _modulation(c8, mod_w, mod_b):
    depth, d, n = mod_w.shape
    tn = _pick(n, (1024, 512, 256, 128))
    return pl.pallas_call(
        _mod_kernel,
        out_shape=jax.ShapeDtypeStruct((depth, SUBLANES, n), F32),
        grid=(depth, n // tn),
        in_specs=[
            pl.BlockSpec((SUBLANES, d), lambda l, j: (0, 0)),
            pl.BlockSpec((None, d, tn), lambda l, j: (l, 0, j)),
            pl.BlockSpec((None, 1, tn), lambda l, j: (l, 0, j)),
        ],
        out_specs=pl.BlockSpec((None, SUBLANES, tn), lambda l, j: (l, 0, j)),
        compiler_params=_cparams(("parallel", "parallel")),
        name="modulation",
    )(c8, mod_w, mod_b.reshape(depth, 1, n))


def _normmod_kernel(x_ref, mod_ref, g_ref, h_ref, *, d):
    x = x_ref[...]
    ms = jnp.mean(x * x, axis=-1, keepdims=True)
    y = x * lax.rsqrt(ms + EPS_RMS) * g_ref[...]
    h_ref[...] = (y * (1.0 + mod_ref[:, d:2 * d]) + mod_ref[:, 0:d]).astype(h_ref.dtype)


def _normmod(x_all, mods_l, g, tm, mod_row):
    m, d = x_all.shape
    return pl.pallas_call(
        functools.partial(_normmod_kernel, d=d),
        out_shape=jax.ShapeDtypeStruct((m, d), BF16),
        grid=(m // tm,),
        in_specs=[
            pl.BlockSpec((tm, d), lambda i: (i, 0)),
            pl.BlockSpec((None, 1, 3 * d), lambda i: (mod_row(i), 0, 0)),
            pl.BlockSpec((1, d), lambda i: (0, 0)),
        ],
        out_specs=pl.BlockSpec((tm, d), lambda i: (i, 0)),
        compiler_params=_cparams(("parallel",)),
        name="normmod",
    )(x_all, mods_l, g)


def _inproj_kernel(h_ref, w_ref, o_ref):
    o_ref[...] = jnp.dot(h_ref[...], w_ref[...], preferred_element_type=F32)


def _inproj(h, w_bf):
    m, d = h.shape
    n = w_bf.shape[1]
    tm = _pick(m, (1024, 512, 256, 128, 64))
    tn = _pick(n, (1664, 1408, 1280, 1024, 768, 640, 512, 384, 256, 128))
    return pl.pallas_call(
        _inproj_kernel,
        out_shape=jax.ShapeDtypeStruct((m, n), F32),
        grid=(n // tn, m // tm),
        in_specs=[
            pl.BlockSpec((tm, d), lambda j, i: (i, 0)),
            pl.BlockSpec((d, tn), lambda j, i: (0, j)),
        ],
        out_specs=pl.BlockSpec((tm, tn), lambda j, i: (i, j)),
        compiler_params=_cparams(("parallel", "parallel")),
        name="inproj",
    )(h, w_bf)


def _outproj_kernel(*refs, n_in, d):
    y_refs = refs[:n_in]
    w_refs = refs[n_in:2 * n_in]
    x_ref, mod_ref, g_ref, o_ref = refs[2 * n_in:]
    acc = jnp.dot(y_refs[0][...], w_refs[0][...], preferred_element_type=F32)
    for i in range(1, n_in):
        acc = acc + jnp.dot(y_refs[i][...], w_refs[i][...], preferred_element_type=F32)
    ms = jnp.mean(acc * acc, axis=-1, keepdims=True)
    z = acc * lax.rsqrt(ms + EPS_RMS) * g_ref[...]
    o_ref[...] = x_ref[...] + mod_ref[:, 2 * d:3 * d] * z


def _outproj(ys, ws, x_all, mods_l, g, tm, mod_row, m_out):
    d = x_all.shape[1]
    n_in = len(ys)
    in_specs = [pl.BlockSpec((tm, y.shape[1]), lambda i: (i, 0)) for y in ys]
    in_specs += [pl.BlockSpec(w.shape, lambda i: (0, 0)) for w in ws]
    in_specs += [
        pl.BlockSpec((tm, d), lambda i: (i, 0)),
        pl.BlockSpec((None, 1, 3 * d), lambda i: (mod_row(i), 0, 0)),
        pl.BlockSpec((1, d), lambda i: (0, 0)),
    ]
    return pl.pallas_call(
        functools.partial(_outproj_kernel, n_in=n_in, d=d),
        out_shape=jax.ShapeDtypeStruct((m_out, d), F32),
        grid=(m_out // tm,),
        in_specs=in_specs,
        out_specs=pl.BlockSpec((tm, d), lambda i: (i, 0)),
        compiler_params=_cparams(("parallel",)),
        name="outproj",
    )(*ys, *ws, x_all, mods_l, g)


class _Geom:
    def __init__(self, batch, seq, ctx):
        self.batch, self.seq, self.ctx = batch, seq, ctx
        self.n_lat = batch * seq
        self.m = batch * (seq + ctx)

    def seq_edges(self, row0, rows):
        in_lat = row0 < self.n_lat
        off = jnp.where(in_lat, row0 % self.seq, (row0 - self.n_lat) % self.ctx)
        length = jnp.where(in_lat, self.seq, self.ctx)
        return off == 0, off + rows == length

    def chunk_block(self, b, c, rows, reverse):
        cc = self.ctx // rows
        sc = self.seq // rows
        if reverse:
            ctx_blk = self.n_lat // rows + b * cc + (cc - 1 - c)
            lat_blk = b * sc + (sc - 1 - (c - cc))
        else:
            ctx_blk = self.n_lat // rows + b * cc + c
            lat_blk = b * sc + (c - cc)
        return jnp.where(c < cc, ctx_blk, lat_blk)


def _shift_rows(x, hp_ref, hn_ref, cols, is_first, is_last, back, fwd):
    rows = x.shape[0]
    r8 = lax.broadcasted_iota(jnp.int32, (SUBLANES, 1), 0)
    hp = jnp.where(is_first, 0.0, hp_ref[:, cols])
    hn = jnp.where(is_last, 0.0, hn_ref[:, cols])
    outs = {}
    for s in range(1, back + 1):
        y = pltpu.roll(x, s, 0)
        head = jnp.where(r8 < s, pltpu.roll(hp, s, 0), y[:SUBLANES])
        outs[-s] = jnp.concatenate([head, y[SUBLANES:]], axis=0)
    for s in range(1, fwd + 1):
        y = pltpu.roll(x, rows - s, 0)
        tail = jnp.where(r8 >= SUBLANES - s, pltpu.roll(hn, SUBLANES - s, 0), y[rows - SUBLANES:])
        outs[s] = jnp.concatenate([y[:rows - SUBLANES], tail], axis=0)
    return outs


def _halo_specs(tp, width, m):
    nb8 = m // SUBLANES
    r8 = tp // SUBLANES
    return [
        pl.BlockSpec((SUBLANES, width), lambda i: (jnp.maximum(i * r8 - 1, 0), 0)),
        pl.BlockSpec((SUBLANES, width), lambda i: (jnp.minimum((i + 1) * r8, nb8 - 1), 0)),
    ]


def _rwkv_prep_kernel(f_ref, hp_ref, hn_ref, mu_ref, w0_ref, wup_ref, a0_ref, aup_ref, kk_ref, ka_ref,
                      sel_ref, selt_ref, r_ref, v_ref, kap_ref, kd_ref, bb_ref, lw_ref, *, geom, tp, da):
    i = pl.program_id(0)
    is_first, is_last = geom.seq_edges(i * tp, tp)

    def shifted(lo, hi):
        cols = slice(lo, hi)
        x = f_ref[:, cols]
        nb = _shift_rows(x, hp_ref, hn_ref, cols, is_first, is_last, 1, 1)
        return x + mu_ref[0:1, cols] * (nb[-1] - x) + mu_ref[1:2, cols] * (nb[1] - x)

    r = shifted(0, da)
    k = shifted(da, 2 * da)
    v = shifted(2 * da, 3 * da)
    cw = jnp.tanh(shifted(3 * da, 3 * da + LANES))
    ca = shifted(3 * da + LANES, 3 * da + 2 * LANES)
    r_ref[...] = r
    v_ref[...] = v
    kk = k * kk_ref[...]
    ss = _dot_sel_r(kk * kk, sel_ref[...])
    inv = lax.rsqrt(ss + 1e-12)
    kap = kk * _dot_sel_r(inv, selt_ref[...])
    kap_ref[...] = kap
    for dd in range(2):
        zw = w0_ref[dd:dd + 1, :] + _bdot(cw, wup_ref[dd])
        lw_ref[dd] = -math.exp(-0.5) * _sigmoid(zw)
        a = _sigmoid(a0_ref[dd:dd + 1, :] + _bdot(ca, aup_ref[dd]))
        kd_ref[dd] = k * (1.0 + (a - 1.0) * ka_ref[...])
        bb_ref[dd] = kap * a


def _rwkv_prep(proj_a, geom, tp, mu, w0, wup_pad, a0, aup_pad, k_k, k_a, sel, selt):
    m, wa = proj_a.shape
    da = k_k.shape[1]
    full = lambda a: pl.BlockSpec(a.shape, lambda i: (0,) * a.ndim)
    tok = jax.ShapeDtypeStruct((m, da), F32)
    tok2 = jax.ShapeDtypeStruct((2, m, da), F32)
    params = (mu, w0, wup_pad, a0, aup_pad, k_k, k_a, sel, selt)
    return pl.pallas_call(
        functools.partial(_rwkv_prep_kernel, geom=geom, tp=tp, da=da),
        out_shape=(tok, tok, tok, tok2, tok2, tok2),
        grid=(m // tp,),
        in_specs=[pl.BlockSpec((tp, wa), lambda i: (i, 0))] + _halo_specs(tp, wa, m) + [full(p) for p in params],
        out_specs=(
            pl.BlockSpec((tp, da), lambda i: (i, 0)),
            pl.BlockSpec((tp, da), lambda i: (i, 0)),
            pl.BlockSpec((tp, da), lambda i: (i, 0)),
            pl.BlockSpec((2, tp, da), lambda i: (0, i, 0)),
            pl.BlockSpec((2, tp, da), lambda i: (0, i, 0)),
            pl.BlockSpec((2, tp, da), lambda i: (0, i, 0)),
        ),
        compiler_params=_cparams(("parallel",)),
        name="rwkv_prep",
    )(proj_a, proj_a, proj_a, *params)


def _rwkv_chunk_terms(jobs):
    L = CHUNK
    each = lambda f, *cols: [f(*a) for a in zip(*cols)]
    r, v, kap, kd, bb, lw, consts = (list(c) for c in zip(*jobs))
    tri_sel, strict, incl, bd_full, m0, m1, eye, last_row, levels = (list(c) for c in zip(*consts))
    stack = lambda x: jnp.concatenate([x * m0[0], x * m1[0]], axis=0).astype(BF16)

    cum = each(_dot_sel_l, tri_sel, lw)
    tot = each(lambda c, lr: c[lr:lr + 1, :], cum, last_row)
    rt = each(lambda x, c: (x * jnp.exp(c)).astype(BF16), r, cum)
    e_neg = each(lambda c: jnp.exp(-c), cum)
    kt = each(lambda x, e: x * e, kd, e_neg)
    bt = each(lambda x, e: x * e, bb, e_neg)
    kapt = each(lambda x, c, l: (x * jnp.exp(c - l)).astype(BF16), kap, cum, lw)
    vv_s = each(stack, v)
    g = each(lambda ka, x, b_, k_: _bdot_nt(jnp.concatenate([ka, x], axis=0),
                                            jnp.concatenate([stack(b_), stack(k_)], axis=0)),
             kapt, rt, bt, kt)
    n = each(lambda m_, x: jnp.where(m_, -x[:L, :2 * L], 0.0), strict, g)
    mk = each(lambda m_, x: jnp.where(m_, x[:L, 2 * L:], 0.0), strict, g)
    arb = each(lambda m_, x: jnp.where(m_, x[L:, :2 * L], 0.0), incl, g)
    ark = each(lambda m_, x: jnp.where(m_, x[L:, 2 * L:], 0.0), incl, g)
    tm = each(lambda e, lv, x: e + jnp.where(lv[0], x, 0.0), eye, levels, n)
    for li in range(1, len(levels[0])):
        x = each(lambda lv, n_, t: _bdot(jnp.where(lv[li], n_, 0.0), stack(t)), levels, n, tm)
        tm = each(lambda t, x_: t + _bdot(t, stack(x_)), tm, x)
    mkv = each(_bdot, mk, vv_s)
    wy = each(lambda t, ka, x: _bdot(t, jnp.concatenate([stack(ka), stack(x)], axis=1)), tm, kapt, mkv)
    z = each(_bdot, ark, vv_s)
    rest = each(lambda t, c: jnp.exp(t - c), tot, cum)
    gk = each(lambda v_, k_, e: _bdot_tn(v_, k_ * e), v, kd, rest)
    wrt = each(lambda x, rt_: jnp.concatenate([x[:, :LANES].astype(BF16), rt_], axis=0), wy, rt)
    y = each(lambda x: x[:, LANES:], wy)
    bp = each(lambda b_, e: (b_ * e).astype(BF16), bb, rest)
    return list(zip(wrt, y, arb, z, bp, gk, tot))


def _rwkv_chunk_states(s_prev, terms, consts):
    L = CHUNK
    each = lambda f, *cols: [f(*a) for a in zip(*cols)]
    wrt, y, arb, z, bp, gk, tot = (list(c) for c in zip(*terms))
    bd_full, m0, m1 = consts[3], consts[4], consts[5]
    stack = lambda x: jnp.concatenate([x * m0, x * m1], axis=0).astype(BF16)
    wr = each(_bdot_nt, wrt, s_prev)
    u = each(lambda x, y_: -(x[:L] + y_), wr, y)
    o = each(lambda x, a, u_, z_: x[L:] + _bdot(a, stack(u_)) + z_, wr, arb, u, z)
    s_new = each(lambda s, t, u_, b_, g_: jnp.where(bd_full, s * jnp.exp(t) + _bdot_tn(u_, b_) + g_, 0.0),
                 s_prev, tot, u, bp, gk)
    return s_new, o


def _rwkv_consts(reverse):
    L = CHUNK
    ri = lax.broadcasted_iota(jnp.int32, (L, 2 * L), 0)
    ci = lax.broadcasted_iota(jnp.int32, (L, 2 * L), 1) & (L - 1)
    strict = (ci > ri) if reverse else (ci < ri)
    incl = (ci >= ri) if reverse else (ci <= ri)
    r2 = lax.broadcasted_iota(jnp.int32, (L, L), 0)
    c2 = lax.broadcasted_iota(jnp.int32, (L, L), 1)
    tri_sel = jnp.where((c2 >= r2) if reverse else (c2 <= r2), 1.0, 0.0).astype(BF16)
    lane = lax.broadcasted_iota(jnp.int32, (1, LANES), 1)
    m0 = jnp.where(lane < HEAD, 1.0, 0.0)
    m1 = 1.0 - m0
    eye = jnp.where(ri == ci, 1.0, 0.0)
    rs = lax.broadcasted_iota(jnp.int32, (2 * HEAD, 2 * HEAD), 0)
    cs = lax.broadcasted_iota(jnp.int32, (2 * HEAD, 2 * HEAD), 1)
    bd_full = (rs >= HEAD) == (cs >= HEAD)
    levels = tuple(((ri >> (l + 1)) == (ci >> (l + 1))) & ((ri >> l) != (ci >> l))
                   for l in range(int(math.log2(L))))
    return (tri_sel, strict, incl, bd_full, m0, m1, eye, 0 if reverse else L - 1, levels)


def _rwkv_scan_kernel(*refs, npairs, nsub):
    ins = refs[:12]
    of_ref, ob_ref, s_ref = refs[12:]

    @pl.when(pl.program_id(2) == 0)
    def _():
        s_ref[...] = jnp.zeros(s_ref.shape, F32)

    consts = [_rwkv_consts(False), _rwkv_consts(True)]
    chains = [(dd, p) for dd in range(2) for p in range(npairs)]
    sub = lambda dd, k: (nsub - 1 - k) if dd == 1 else k
    jobs = []
    for k in range(nsub):
        for dd, p in chains:
            r_ref, v_ref, kap_ref, kd_ref, bb_ref, lw_ref = ins[6 * dd:6 * dd + 6]
            rs = slice(sub(dd, k) * CHUNK, (sub(dd, k) + 1) * CHUNK)
            sl = slice(p * LANES, (p + 1) * LANES)
            jobs.append((r_ref[rs, sl], v_ref[rs, sl], kap_ref[rs, sl], kd_ref[rs, sl], bb_ref[rs, sl],
                         lw_ref[rs, sl], consts[dd]))
    terms = _rwkv_chunk_terms(jobs)
    states = [s_ref[dd, p] for dd, p in chains]
    for k in range(nsub):
        states, outs = _rwkv_chunk_states(states, terms[k * len(chains):(k + 1) * len(chains)], consts[0])
        for (dd, p), o in zip(chains, outs):
            o_ref = ob_ref if dd == 1 else of_ref
            o_ref[sub(dd, k) * CHUNK:(sub(dd, k) + 1) * CHUNK, p * LANES:(p + 1) * LANES] = o
    for (dd, p), s_new in zip(chains, states):
        s_ref[dd, p] = s_new


def _rwkv_scan(r, v, kap, kd, bb, lw, geom):
    m, da = r.shape
    gw = _pick(da, (512, 256, 128))
    npairs = gw // LANES
    rows = _pick(geom.ctx, (4 * CHUNK, 2 * CHUNK, CHUNK))
    nsteps = (geom.seq + geom.ctx) // rows
    in_specs = []
    for dd in range(2):
        blk = lambda b, g, c, dd=dd: geom.chunk_block(b, c, rows, dd == 1)
        tok = pl.BlockSpec((rows, gw), lambda b, g, c, blk=blk: (blk(b, g, c), g))
        tokd = pl.BlockSpec((None, rows, gw), lambda b, g, c, blk=blk, dd=dd: (dd, blk(b, g, c), g))
        in_specs += [tok, tok, tok, tokd, tokd, tokd]
    out_specs = tuple(
        pl.BlockSpec((rows, gw), lambda b, g, c, dd=dd: (geom.chunk_block(b, c, rows, dd == 1), g))
        for dd in range(2))
    tok_shape = jax.ShapeDtypeStruct((m, da), F32)
    return pl.pallas_call(
        functools.partial(_rwkv_scan_kernel, npairs=npairs, nsub=rows // CHUNK),
        out_shape=(tok_shape, tok_shape),
        grid=(geom.batch, da // gw, nsteps),
        in_specs=in_specs,
        out_specs=out_specs,
        scratch_shapes=[pltpu.VMEM((2, npairs, LANES, LANES), F32)],
        compiler_params=_cparams(("parallel", "parallel", "arbitrary")),
        name="rwkv_scan",
    )(r, v, kap, kd, bb, lw, r, v, kap, kd, bb, lw)


def _rwkv_readout_kernel(of_ref, ob_ref, r_ref, v_ref, kd_ref, g_ref, rk_ref, gw_ref, gb_ref,
                         sel_ref, selt_ref, y_ref):
    sel, selt = sel_ref[...], selt_ref[...]
    head_mean = lambda x: _dot_sel_r(_dot_sel_r(x, sel) * (1.0 / HEAD), selt)
    o = of_ref[...] + ob_ref[...]
    cen = o - head_mean(o)
    var = head_mean(cen * cen)
    on = cen * lax.rsqrt(var + GN_EPS) * gw_ref[...] + gb_ref[...]
    bonus = _dot_sel_r(_dot_sel_r(r_ref[...] * (kd_ref[0] + kd_ref[1]) * rk_ref[...], sel), selt)
    y_ref[...] = ((on + bonus * v_ref[...]) * _silu(g_ref[...])).astype(y_ref.dtype)


def _rwkv_readout(o_f, o_b, r, v, kd, proj_b, r_k, gn_w, gn_b, sel, selt, tr):
    m, da = r.shape
    full = lambda a: pl.BlockSpec(a.shape, lambda i: (0,) * a.ndim)
    tok = pl.BlockSpec((tr, da), lambda i: (i, 0))
    params = (r_k, gn_w, gn_b, sel, selt)
    return pl.pallas_call(
        _rwkv_readout_kernel,
        out_shape=jax.ShapeDtypeStruct((m, da), BF16),
        grid=(m // tr,),
        in_specs=[tok, tok, tok, tok, pl.BlockSpec((2, tr, da), lambda i: (0, i, 0)), tok] + [full(p) for p in params],
        out_specs=tok,
        compiler_params=_cparams(("parallel",)),
        name="rwkv_readout",
    )(o_f, o_b, r, v, kd, proj_b, *params)


def _attn_rows(qs, k_loc, v_loc, bias, kc, vc, m0):
    each = lambda f, *cols: [f(*a) for a in zip(*cols)]
    n = qs[0].shape[0]
    q2 = each(lambda q: (jnp.concatenate([q * m0, q * (1.0 - m0)], axis=0) * (HEAD ** -0.5)).astype(BF16), qs)
    s_ctx = each(lambda q: _bdot_nt(q, kc), q2)
    mx = each(lambda s: jnp.max(s, axis=-1, keepdims=True), s_ctx)
    if k_loc is not None:
        s_loc = each(lambda q, k, b: _bdot_nt(q, k) + b, q2, k_loc, bias)
        mx = each(lambda m_, s: jnp.maximum(m_, jnp.max(s, axis=-1, keepdims=True)), mx, s_loc)
        p_loc = each(lambda s, m_: jnp.exp(s - m_), s_loc, mx)
    p_ctx = each(lambda s, m_: jnp.exp(s - m_), s_ctx, mx)
    den = each(lambda p: jnp.sum(p, axis=-1, keepdims=True), p_ctx)
    acc = each(lambda p: _bdot(p, vc), p_ctx)
    if k_loc is not None:
        den = each(lambda d_, p: d_ + jnp.sum(p, axis=-1, keepdims=True), den, p_loc)
        acc = each(lambda a, p, v: a + _bdot(p, v), acc, p_loc, v_loc)
    acc = each(lambda a, d_: a / d_, acc, den)
    return each(lambda a: jnp.where(m0 > 0.5, a[:n], a[n:]), acc)


def _na_kernel(q_ref, g_ref, k_ref, v_ref, kc_ref, vc_ref, bias_ref, y_ref, kb_ref, vb_ref, kcb_ref, vcb_ref,
               *, rows_per_step, img_rows, n_lat_steps):
    step = pl.program_id(2)
    lane = lax.broadcasted_iota(jnp.int32, (1, LANES), 1)
    m0 = jnp.where(lane < HEAD, 1.0, 0.0)
    kh = min(NA_WIN_H, img_rows)

    @pl.when(step == 0)
    def _():
        kb_ref[...] = k_ref[...].astype(BF16)
        vb_ref[...] = v_ref[...].astype(BF16)
        kcb_ref[...] = kc_ref[...].astype(BF16)
        vcb_ref[...] = vc_ref[...].astype(BF16)

    kc = kcb_ref[...]
    vc = vcb_ref[...]

    @pl.when(step < n_lat_steps)
    def _():
        qs, k_loc, v_loc, bias = [], [], [], []
        for yy in range(rows_per_step):
            y = step * rows_per_step + yy
            y0 = jnp.clip(y - kh // 2, 0, img_rows - kh)
            start = pl.multiple_of(y0 * GRID_W, GRID_W)
            k_loc.append(kb_ref[pl.ds(start, kh * GRID_W), :])
            v_loc.append(vb_ref[pl.ds(start, kh * GRID_W), :])
            bias.append(bias_ref[:, y - y0].reshape(2 * GRID_W, kh * GRID_W))
            qs.append(q_ref[yy * GRID_W:(yy + 1) * GRID_W, :])
        outs = _attn_rows(qs, k_loc, v_loc, bias, kc, vc, m0)
        for yy, o in enumerate(outs):
            rs = slice(yy * GRID_W, (yy + 1) * GRID_W)
            y_ref[rs, :] = (o * _silu(g_ref[rs, :])).astype(y_ref.dtype)

    @pl.when(step >= n_lat_steps)
    def _():
        o, = _attn_rows([q_ref[...]], None, None, None, kc, vc, m0)
        y_ref[...] = (o * _silu(g_ref[...])).astype(y_ref.dtype)


def _na_bias_table(rpb, img_rows):
    kh = min(NA_WIN_H, img_rows)
    cols = np.arange(GRID_W)
    col_start = np.clip(cols - NA_WIN_W // 2, 0, GRID_W - NA_WIN_W)
    kx = np.arange(GRID_W)
    inwin = (kx[None, :] >= col_start[:, None]) & (kx[None, :] < col_start[:, None] + NA_WIN_W)
    dx = kx[None, :] - cols[:, None] + (NA_WIN_W - 1)
    onehot = (np.arange(2 * NA_WIN_W - 1)[:, None, None] == dx[None]) & inwin[None]
    toep = jnp.einsum('hyd,dxk->hyxk', rpb, jnp.asarray(onehot, rpb.dtype), precision=lax.Precision.HIGHEST)
    toep = jnp.where(inwin[None, None], toep, NEG_BIG)
    tabs = [toep[:, NA_WIN_H - 1 - dl:NA_WIN_H - 1 - dl + kh] for dl in range(kh)]
    tab = jnp.stack(tabs, axis=1).transpose(0, 1, 3, 2, 4)
    return tab.reshape(rpb.shape[0], kh, GRID_W, kh * GRID_W)


def _na(proj_b, bias_tab, geom, db):
    m = proj_b.shape[0]
    npair = db // LANES
    img_rows = geom.seq // GRID_W
    rb = geom.ctx
    rows_per_step = rb // GRID_W
    n_lat_steps = geom.seq // rb
    nblk = db // LANES
    def qrow(b, p, s):
        return jnp.where(s < n_lat_steps, b * n_lat_steps + s, geom.n_lat // rb + b)
    kh = min(NA_WIN_H, img_rows)
    return pl.pallas_call(
        functools.partial(_na_kernel, rows_per_step=rows_per_step, img_rows=img_rows, n_lat_steps=n_lat_steps),
        out_shape=jax.ShapeDtypeStruct((m, db), BF16),
        grid=(geom.batch, npair, n_lat_steps + 1),
        in_specs=[
            pl.BlockSpec((rb, LANES), lambda b, p, s: (qrow(b, p, s), 1 * nblk + p)),
            pl.BlockSpec((rb, LANES), lambda b, p, s: (qrow(b, p, s), 4 * nblk + p)),
            pl.BlockSpec((geom.seq, LANES), lambda b, p, s: (b, 2 * nblk + p)),
            pl.BlockSpec((geom.seq, LANES), lambda b, p, s: (b, 3 * nblk + p)),
            pl.BlockSpec((geom.ctx, LANES), lambda b, p, s: (geom.n_lat // geom.ctx + b, 2 * nblk + p)),
            pl.BlockSpec((geom.ctx, LANES), lambda b, p, s: (geom.n_lat // geom.ctx + b, 3 * nblk + p)),
            pl.BlockSpec((2, kh, GRID_W, kh * GRID_W), lambda b, p, s: (p, 0, 0, 0)),
        ],
        out_specs=pl.BlockSpec((rb, LANES), lambda b, p, s: (qrow(b, p, s), p)),
        scratch_shapes=[pltpu.VMEM((geom.seq, LANES), BF16), pltpu.VMEM((geom.seq, LANES), BF16),
                        pltpu.VMEM((geom.ctx, LANES), BF16), pltpu.VMEM((geom.ctx, LANES), BF16)],
        compiler_params=_cparams(("parallel", "parallel", "arbitrary")),
        name="nbr_attention",
    )(proj_b, proj_b, proj_b, proj_b, proj_b, proj_b, bias_tab)


def _rglru_kernel(*refs, geom, tc, gwc, reverse):
    if reverse:
        (x_ref, hp_ref, hn_ref, cw_ref, cb_ref, wa_ref, ba_ref, wx_ref, bx_ref, lam_ref, hf_ref, g_ref,
         out_ref, carry_ref) = refs
    else:
        (x_ref, hp_ref, hn_ref, cw_ref, cb_ref, wa_ref, ba_ref, wx_ref, bx_ref, lam_ref,
         out_ref, carry_ref) = refs
    b, c = pl.program_id(0), pl.program_id(2)

    @pl.when(c == 0)
    def _():
        carry_ref[...] = jnp.zeros(carry_ref.shape, F32)

    row0 = geom.chunk_block(b, c, tc, reverse) * tc
    is_first, is_last = geom.seq_edges(row0, tc)
    x = x_ref[...]
    nb = _shift_rows(x, hp_ref, hn_ref, slice(None), is_first, is_last, CONV_LEFT, CONV_W - 1 - CONV_LEFT)
    nb[0] = x
    u = cb_ref[...]
    for j in range(CONV_W):
        u = u + cw_ref[j:j + 1, :] * nb[j - CONV_LEFT]
    gate_r = _sigmoid(_bdot(u, wa_ref[0]) + ba_ref[...])
    gate_i = _sigmoid(_bdot(u, wx_ref[0]) + bx_ref[...])
    neg_lam = -lam_ref[...]
    softplus = jnp.maximum(neg_lam, 0.0) + jnp.log1p(jnp.exp(-jnp.abs(neg_lam)))
    log_a = (-RGLRU_C) * gate_r * softplus
    a = jnp.exp(log_a)
    th = jnp.tanh(log_a)
    bv = jnp.sqrt(-2.0 * th / (1.0 - th)) * (gate_i * u)
    ng = tc // SUBLANES
    a = a.reshape(ng, SUBLANES, gwc)
    bv = bv.reshape(ng, SUBLANES, gwc)
    r8 = lax.broadcasted_iota(jnp.int32, (1, SUBLANES, 1), 1)
    s = 1
    while s < SUBLANES:
        if reverse:
            a_s, b_s = pltpu.roll(a, SUBLANES - s, 1), pltpu.roll(bv, SUBLANES - s, 1)
            valid = r8 < SUBLANES - s
        else:
            a_s, b_s = pltpu.roll(a, s, 1), pltpu.roll(bv, s, 1)
            valid = r8 >= s
        bv = jnp.where(valid, a * b_s + bv, bv)
        a = jnp.where(valid, a * a_s, a)
        s *= 2
    a = a.reshape(tc, gwc)
    bv = bv.reshape(tc, gwc)
    carry = carry_ref[0:1, :]
    last = 0 if reverse else SUBLANES - 1
    groups = [None] * (tc // SUBLANES)
    for gi in (reversed(range(len(groups))) if reverse else range(len(groups))):
        rs = slice(gi * SUBLANES, (gi + 1) * SUBLANES)
        groups[gi] = bv[rs] + a[rs] * carry
        carry = groups[gi][last:last + 1, :]
    h = jnp.concatenate(groups, axis=0)
    carry_ref[...] = jnp.broadcast_to(carry, carry_ref.shape)
    if reverse:
        out_ref[...] = ((hf_ref[...] + h) * _silu(g_ref[...])).astype(out_ref.dtype)
    else:
        out_ref[...] = h


def _rglru(proj_c, geom, tc, gwc, conv_w, conv_b, wa_g, ba, wx_g, bx, lam, reverse, h_f=None):
    m = proj_c.shape[0]
    dc = conv_w.shape[1]
    ngrp = dc // gwc
    nch = (geom.seq + geom.ctx) // tc
    nb8 = m // SUBLANES
    r8 = tc // SUBLANES
    blk = lambda b, g, c: geom.chunk_block(b, c, tc, reverse)
    vec = lambda rows: pl.BlockSpec((rows, gwc), lambda b, g, c: (0, g))
    in_specs = [
        pl.BlockSpec((tc, gwc), lambda b, g, c: (blk(b, g, c), g)),
        pl.BlockSpec((SUBLANES, gwc), lambda b, g, c: (jnp.maximum(blk(b, g, c) * r8 - 1, 0), g)),
        pl.BlockSpec((SUBLANES, gwc), lambda b, g, c: (jnp.minimum((blk(b, g, c) + 1) * r8, nb8 - 1), g)),
        vec(CONV_W), vec(1),
        pl.BlockSpec((1, gwc, gwc), lambda b, g, c: (g, 0, 0)), vec(1),
        pl.BlockSpec((1, gwc, gwc), lambda b, g, c: (g, 0, 0)), vec(1),
        vec(1),
    ]
    args = [proj_c, proj_c, proj_c, conv_w, conv_b, wa_g, ba, wx_g, bx, lam]
    if reverse:
        in_specs += [
            pl.BlockSpec((tc, gwc), lambda b, g, c: (blk(b, g, c), g)),
            pl.BlockSpec((tc, gwc), lambda b, g, c: (blk(b, g, c), ngrp + g)),
        ]
        args += [h_f, proj_c]
    return pl.pallas_call(
        functools.partial(_rglru_kernel, geom=geom, tc=tc, gwc=gwc, reverse=reverse),
        out_shape=jax.ShapeDtypeStruct((m, dc), BF16 if reverse else F32),
        grid=(geom.batch, ngrp, nch),
        in_specs=in_specs,
        out_specs=pl.BlockSpec((tc, gwc), lambda b, g, c: (blk(b, g, c), g)),
        scratch_shapes=[pltpu.VMEM((SUBLANES, gwc), F32)],
        compiler_params=_cparams(("parallel", "parallel", "arbitrary")),
        name="rglru_bwd" if reverse else "rglru_fwd",
    )(*args)


def _group_blockdiag(w, gwc):
    nblk, bs, _ = w.shape
    per = gwc // bs
    wg = w.reshape(nblk // per, per, bs, bs)
    eye = jnp.eye(per, dtype=w.dtype)
    dense = jnp.einsum('gpij,pq->gpiqj', wg, eye)
    return dense.reshape(nblk // per, gwc, gwc)


def kernel(x, c, ctx, c_ctx, mod_w, mod_b, norm_pre, norm_post, ev_w_in, ev_mu, ev_w0, ev_w_up, ev_a0, ev_a_up, ev_k_k, ev_k_a, ev_r_k, ev_gn_w, ev_gn_b, ev_rpb, ev_w_out, od_w_in, od_conv_w, od_conv_b, od_gate_a_w, od_gate_a_b, od_gate_x_w, od_gate_x_b, od_lambda, od_w_out):
    batch, seq, d = x.shape
    n_ctx = ctx.shape[1]
    depth = mod_w.shape[0]
    da = ev_k_k.shape[1]
    rw = ev_w_up.shape[2]
    db = ev_w_out.shape[1] - da
    dc = od_conv_w.shape[2]
    rg_bs = od_gate_a_w.shape[-1]
    assert da % LANES == 0 and db == da and 2 * rw == LANES and ev_a_up.shape[2] == rw
    assert batch < SUBLANES and seq % GRID_W == 0 and seq % n_ctx == 0 and n_ctx % CHUNK == 0
    assert n_ctx % GRID_W == 0 and (batch * seq) % n_ctx == 0
    geom = _Geom(batch, seq, n_ctx)
    m, n_lat = geom.m, geom.n_lat

    tm = _pick(math.gcd(n_lat, batch * n_ctx), (512, 256, 128, 64))
    tp = _pick(math.gcd(seq, n_ctx), (256, 128, 64))
    lat_blocks = n_lat // tm
    per_batch = seq // tm
    mod_row = lambda i: jnp.where(i < lat_blocks, i // per_batch, batch)

    c8 = jnp.zeros((SUBLANES, d), F32).at[:batch].set(c).at[batch].set(c_ctx)
    mods = _modulation(c8, mod_w, mod_b).reshape(depth, SUBLANES, 1, 3 * d)

    x_all = jnp.concatenate([x.reshape(n_lat, d), ctx.reshape(batch * n_ctx, d)], axis=0)

    heads_a = da // HEAD
    sel_np = np.zeros((da, LANES), np.float32)
    sel_np[np.arange(da), np.arange(da) // HEAD] = 1.0
    sel = jnp.asarray(sel_np, BF16)
    selt = jnp.asarray(sel_np.T, BF16)
    assert heads_a <= LANES

    a_shifted = 3 * da + 4 * rw
    gwc = rg_bs * LANES // math.gcd(rg_bs, LANES)

    for layer in range(depth):
        last = layer == depth - 1
        i = layer // 2
        mods_l = mods[layer]
        g_pre = norm_pre[layer].reshape(1, d)
        g_post = norm_post[layer].reshape(1, d)
        m_out = n_lat if last else m
        h = _normmod(x_all, mods_l, g_pre, tm, mod_row)
        if layer % 2 == 0:
            w_in = ev_w_in[i].astype(BF16)
            proj_a = _inproj(h, w_in[:, :a_shifted])
            proj_b = _inproj(h, w_in[:, a_shifted:])
            pad_dir = lambda w: jnp.stack([
                jnp.concatenate([w[0], jnp.zeros_like(w[1])], axis=0),
                jnp.concatenate([jnp.zeros_like(w[0]), w[1]], axis=0)]).astype(BF16)
            r, v, kap, kd, bb, lw = _rwkv_prep(
                proj_a, geom, tp, ev_mu[i], ev_w0[i], pad_dir(ev_w_up[i]), ev_a0[i], pad_dir(ev_a_up[i]),
                ev_k_k[i].reshape(1, da), ev_k_a[i].reshape(1, da), sel, selt)
            o_f, o_b = _rwkv_scan(r, v, kap, kd, bb, lw, geom)
            y_a = _rwkv_readout(o_f, o_b, r, v, kd, proj_b, ev_r_k[i].reshape(1, da),
                                ev_gn_w[i].reshape(1, da), ev_gn_b[i].reshape(1, da), sel, selt, tp)
            y_b = _na(proj_b, _na_bias_table(ev_rpb[i], seq // GRID_W), geom, db)
            w_out = ev_w_out[i].astype(BF16)
            x_all = _outproj([y_a, y_b], [w_out[:da], w_out[da:]], x_all, mods_l, g_post, tm, mod_row, m_out)
        else:
            proj_c = _inproj(h, od_w_in[i].astype(BF16))
            row = lambda a: a.reshape(1, dc)
            h_f = None
            for dd in range(2):
                out = _rglru(proj_c, geom, tp, gwc, od_conv_w[i], row(od_conv_b[i]),
                             _group_blockdiag(od_gate_a_w[i, dd], gwc).astype(BF16), row(od_gate_a_b[i, dd]),
                             _group_blockdiag(od_gate_x_w[i, dd], gwc).astype(BF16), row(od_gate_x_b[i, dd]),
                             row(od_lambda[i, dd]), dd == 1, h_f)
                h_f = out
            x_all = _outproj([out], [od_w_out[i].astype(BF16)], x_all, mods_l, g_post, tm, mod_row, m_out)
    return x_all.reshape(batch, seq, d)
```

```python
import functools
import math

import jax
import jax.numpy as jnp
import numpy as np
from jax import lax
from jax.experimental import pallas as pl
from jax.experimental.pallas import tpu as pltpu

F32 = jnp.float32
BF16 = jnp.bfloat16

EPS_RMS = 1e-6
GN_EPS = 64e-5
RGLRU_C = 8.0
HEAD = 64
GRID_W = 64
NA_WIN_H = 8
NA_WIN_W = 16
CONV_W = 4
CONV_LEFT = 2
LANES = 128
SUBLANES = 8
SEG = 4
CHUNK = 64
NEG_BIG = -1e30
VMEM_LIMIT = 56 * 1024 * 1024


def _cparams(sem):
    return pltpu.CompilerParams(dimension_semantics=sem, vmem_limit_bytes=VMEM_LIMIT)


def _pick(n, cands):
    for c in cands:
        if n % c == 0:
            return c
    raise ValueError(f"no tile for {n} in {cands}")


def _bdot(a, b):
    return jnp.dot(a.astype(BF16), b.astype(BF16), preferred_element_type=F32)


def _bdot_nt(a, b):
    return lax.dot_general(a.astype(BF16), b.astype(BF16), (((1,), (1,)), ((), ())),
                           preferred_element_type=F32)


def _bdot_tn(a, b):
    return lax.dot_general(a.astype(BF16), b.astype(BF16), (((0,), (0,)), ((), ())),
                           preferred_element_type=F32)


def _split3(x):
    hi = x.astype(BF16)
    r1 = x - hi.astype(F32)
    mid = r1.astype(BF16)
    lo = (r1 - mid.astype(F32)).astype(BF16)
    return hi, mid, lo


def _dot_sel_r(x, sel):
    hi, mid, lo = _split3(x)
    d = lambda t: jnp.dot(t, sel, preferred_element_type=F32)
    return d(hi) + d(mid) + d(lo)


def _dot_sel_l(sel, x):
    hi, mid, lo = _split3(x)
    d = lambda t: jnp.dot(sel, t, preferred_element_type=F32)
    return d(hi) + d(mid) + d(lo)


def _sigmoid(x):
    return 1.0 / (1.0 + jnp.exp(-x))


def _silu(x):
    return x * _sigmoid(x)


def _mod_kernel(c_ref, w_ref, b_ref, o_ref):
    cs = _silu(c_ref[...])
    o_ref[...] = _bdot(cs, w_ref[...]) + b_ref[...]


def _modulation(c8, mod_w, mod_b):
    depth, d, n = mod_w.shape
    tn = _pick(n, (1024, 512, 256, 128))
    return pl.pallas_call(
        _mod_kernel,
        out_shape=jax.ShapeDtypeStruct((depth, SUBLANES, n), F32),
        grid=(depth, n // tn),
        in_specs=[
            pl.BlockSpec((SUBLANES, d), lambda l, j: (0, 0)),
            pl.BlockSpec((None, d, tn), lambda l, j: (l, 0, j)),
            pl.BlockSpec((None, 1, tn), lambda l, j: (l, 0, j)),
        ],
        out_specs=pl.BlockSpec((None, SUBLANES, tn), lambda l, j: (l, 0, j)),
        compiler_params=_cparams(("parallel", "parallel")),
        name="modulation",
    )(c8, mod_w, mod_b.reshape(depth, 1, n))


def _normmod_kernel(x_ref, mod_ref, g_ref, h_ref, *, d):
    x = x_ref[...]
    ms = jnp.mean(x * x, axis=-1, keepdims=True)
    y = x * lax.rsqrt(ms + EPS_RMS) * g_ref[...]
    h_ref[...] = (y * (1.0 + mod_ref[:, d:2 * d]) + mod_ref[:, 0:d]).astype(h_ref.dtype)


def _normmod(x_all, mods_l, g, tm, mod_row):
    m, d = x_all.shape
    return pl.pallas_call(
        functools.partial(_normmod_kernel, d=d),
        out_shape=jax.ShapeDtypeStruct((m, d), BF16),
        grid=(m // tm,),
        in_specs=[
            pl.BlockSpec((tm, d), lambda i: (i, 0)),
            pl.BlockSpec((None, 1, 3 * d), lambda i: (mod_row(i), 0, 0)),
            pl.BlockSpec((1, d), lambda i: (0, 0)),
        ],
        out_specs=pl.BlockSpec((tm, d), lambda i: (i, 0)),
        compiler_params=_cparams(("parallel",)),
        name="normmod",
    )(x_all, mods_l, g)


def _inproj_kernel(h_ref, w_ref, o_ref):
    o_ref[...] = jnp.dot(h_ref[...], w_ref[...], preferred_element_type=F32)


def _inproj(h, w_bf):
    m, d = h.shape
    n = w_bf.shape[1]
    tm = _pick(m, (1024, 512, 256, 128, 64))
    tn = _pick(n, (1664, 1408, 1280, 1024, 768, 640, 512, 384, 256, 128))
    return pl.pallas_call(
        _inproj_kernel,
        out_shape=jax.ShapeDtypeStruct((m, n), F32),
        grid=(n // tn, m // tm),
        in_specs=[
            pl.BlockSpec((tm, d), lambda j, i: (i, 0)),
            pl.BlockSpec((d, tn), lambda j, i: (0, j)),
        ],
        out_specs=pl.BlockSpec((tm, tn), lambda j, i: (i, j)),
        compiler_params=_cparams(("parallel", "parallel")),
        name="inproj",
    )(h, w_bf)


def _outproj_kernel(*refs, n_in, d):
    y_refs = refs[:n_in]
    w_refs = refs[n_in:2 * n_in]
    x_ref, mod_ref, g_ref, o_ref = refs[2 * n_in:]
    acc = jnp.dot(y_refs[0][...], w_refs[0][...], preferred_element_type=F32)
    for i in range(1, n_in):
        acc = acc + jnp.dot(y_refs[i][...], w_refs[i][...], preferred_element_type=F32)
    ms = jnp.mean(acc * acc, axis=-1, keepdims=True)
    z = acc * lax.rsqrt(ms + EPS_RMS) * g_ref[...]
    o_ref[...] = x_ref[...] + mod_ref[:, 2 * d:3 * d] * z


def _outproj(ys, ws, x_all, mods_l, g, tm, mod_row, m_out):
    d = x_all.shape[1]
    n_in = len(ys)
    in_specs = [pl.BlockSpec((tm, y.shape[1]), lambda i: (i, 0)) for y in ys]
    in_specs += [pl.BlockSpec(w.shape, lambda i: (0, 0)) for w in ws]
    in_specs += [
        pl.BlockSpec((tm, d), lambda i: (i, 0)),
        pl.BlockSpec((None, 1, 3 * d), lambda i: (mod_row(i), 0, 0)),
        pl.BlockSpec((1, d), lambda i: (0, 0)),
    ]
    return pl.pallas_call(
        functools.partial(_outproj_kernel, n_in=n_in, d=d),
        out_shape=jax.ShapeDtypeStruct((m_out, d), F32),
        grid=(m_out // tm,),
        in_specs=in_specs,
        out_specs=pl.BlockSpec((tm, d), lambda i: (i, 0)),
        compiler_params=_cparams(("parallel",)),
        name="outproj",
    )(*ys, *ws, x_all, mods_l, g)


class _Geom:
    def __init__(self, batch, seq, ctx):
        self.batch, self.seq, self.ctx = batch, seq, ctx
        self.n_lat = batch * seq
        self.m = batch * (seq + ctx)

    def seq_edges(self, row0, rows):
        in_lat = row0 < self.n_lat
        off = jnp.where(in_lat, row0 % self.seq, (row0 - self.n_lat) % self.ctx)
        length = jnp.where(in_lat, self.seq, self.ctx)
        return off == 0, off + rows == length

    def chunk_block(self, b, c, rows, reverse):
        cc = self.ctx // rows
        sc = self.seq // rows
        if reverse:
            ctx_blk = self.n_lat // rows + b * cc + (cc - 1 - c)
            lat_blk = b * sc + (sc - 1 - (c - cc))
        else:
            ctx_blk = self.n_lat // rows + b * cc + c
            lat_blk = b * sc + (c - cc)
        return jnp.where(c < cc, ctx_blk, lat_blk)


def _shift_rows(x, hp_ref, hn_ref, cols, is_first, is_last, back, fwd):
    rows = x.shape[0]
    r8 = lax.broadcasted_iota(jnp.int32, (SUBLANES, 1), 0)
    hp = jnp.where(is_first, 0.0, hp_ref[:, cols])
    hn = jnp.where(is_last, 0.0, hn_ref[:, cols])
    outs = {}
    for s in range(1, back + 1):
        y = pltpu.roll(x, s, 0)
        head = jnp.where(r8 < s, pltpu.roll(hp, s, 0), y[:SUBLANES])
        outs[-s] = jnp.concatenate([head, y[SUBLANES:]], axis=0)
    for s in range(1, fwd + 1):
        y = pltpu.roll(x, rows - s, 0)
        tail = jnp.where(r8 >= SUBLANES - s, pltpu.roll(hn, SUBLANES - s, 0), y[rows - SUBLANES:])
        outs[s] = jnp.concatenate([y[:rows - SUBLANES], tail], axis=0)
    return outs


def _halo_specs(tp, width, m):
    nb8 = m // SUBLANES
    r8 = tp // SUBLANES
    return [
        pl.BlockSpec((SUBLANES, width), lambda i: (jnp.maximum(i * r8 - 1, 0), 0)),
        pl.BlockSpec((SUBLANES, width), lambda i: (jnp.minimum((i + 1) * r8, nb8 - 1), 0)),
    ]


def _rwkv_prep_kernel(f_ref, hp_ref, hn_ref, mu_ref, w0_ref, wup_ref, a0_ref, aup_ref, kk_ref, ka_ref,
                      sel_ref, selt_ref, r_ref, v_ref, kap_ref, kd_ref, bb_ref, lw_ref, *, geom, tp, da):
    i = pl.program_id(0)
    is_first, is_last = geom.seq_edges(i * tp, tp)

    def shifted(lo, hi):
        cols = slice(lo, hi)
        x = f_ref[:, cols]
        nb = _shift_rows(x, hp_ref, hn_ref, cols, is_first, is_last, 1, 1)
        return x + mu_ref[0:1, cols] * (nb[-1] - x) + mu_ref[1:2, cols] * (nb[1] - x)

    r = shifted(0, da)
    k = shifted(da, 2 * da)
    v = shifted(2 * da, 3 * da)
    cw = jnp.tanh(shifted(3 * da, 3 * da + LANES))
    ca = shifted(3 * da + LANES, 3 * da + 2 * LANES)
    r_ref[...] = r.astype(r_ref.dtype)
    v_ref[...] = v.astype(v_ref.dtype)
    kk = k * kk_ref[...]
    ss = _dot_sel_r(kk * kk, sel_ref[...])
    inv = lax.rsqrt(ss + 1e-12)
    kap = kk * _dot_sel_r(inv, selt_ref[...])
    kap_ref[...] = kap.astype(kap_ref.dtype)
    for dd in range(2):
        zw = w0_ref[dd:dd + 1, :] + _bdot(cw, wup_ref[dd])
        lw_ref[dd] = -math.exp(-0.5) * _sigmoid(zw)
        a = _sigmoid(a0_ref[dd:dd + 1, :] + _bdot(ca, aup_ref[dd]))
        kd_ref[dd] = (k * (1.0 + (a - 1.0) * ka_ref[...])).astype(kd_ref.dtype)
        bb_ref[dd] = (kap * a).astype(bb_ref.dtype)


def _rwkv_prep(proj_a, geom, tp, mu, w0, wup_pad, a0, aup_pad, k_k, k_a, sel, selt):
    m, wa = proj_a.shape
    da = k_k.shape[1]
    full = lambda a: pl.BlockSpec(a.shape, lambda i: (0,) * a.ndim)
    tok = jax.ShapeDtypeStruct((m, da), BF16)
    tok2 = jax.ShapeDtypeStruct((2, m, da), BF16)
    params = (mu, w0, wup_pad, a0, aup_pad, k_k, k_a, sel, selt)
    return pl.pallas_call(
        functools.partial(_rwkv_prep_kernel, geom=geom, tp=tp, da=da),
        out_shape=(tok, tok, tok, tok2, tok2, jax.ShapeDtypeStruct((2, m, da), F32)),
        grid=(m // tp,),
        in_specs=[pl.BlockSpec((tp, wa), lambda i: (i, 0))] + _halo_specs(tp, wa, m) + [full(p) for p in params],
        out_specs=(
            pl.BlockSpec((tp, da), lambda i: (i, 0)),
            pl.BlockSpec((tp, da), lambda i: (i, 0)),
            pl.BlockSpec((tp, da), lambda i: (i, 0)),
            pl.BlockSpec((2, tp, da), lambda i: (0, i, 0)),
            pl.BlockSpec((2, tp, da), lambda i: (0, i, 0)),
            pl.BlockSpec((2, tp, da), lambda i: (0, i, 0)),
        ),
        compiler_params=_cparams(("parallel",)),
        name="rwkv_prep",
    )(proj_a, proj_a, proj_a, *params)


def _rwkv_chunk_terms(jobs):
    L = CHUNK
    each = lambda f, *cols: [f(*a) for a in zip(*cols)]
    r, v, kap, kd, bb, lw, consts = (list(c) for c in zip(*jobs))
    tri_sel, strict, incl, bd_full, m0, m1, eye, last_row, levels = (list(c) for c in zip(*consts))
    stack = lambda x: jnp.concatenate([x * m0[0], x * m1[0]], axis=0).astype(BF16)

    cum = each(_dot_sel_l, tri_sel, lw)
    tot = each(lambda c, lr: c[lr:lr + 1, :], cum, last_row)
    rt = each(lambda x, c: (x * jnp.exp(c)).astype(BF16), r, cum)
    e_neg = each(lambda c: jnp.exp(-c), cum)
    kt = each(lambda x, e: x * e, kd, e_neg)
    bt = each(lambda x, e: x * e, bb, e_neg)
    kapt = each(lambda x, c, l: (x * jnp.exp(c - l)).astype(BF16), kap, cum, lw)
    vv_s = each(stack, v)
    g = each(lambda ka, x, b_, k_: _bdot_nt(jnp.concatenate([ka, x], axis=0),
                                            jnp.concatenate([stack(b_), stack(k_)], axis=0)),
             kapt, rt, bt, kt)
    n = each(lambda m_, x: jnp.where(m_, -x[:L, :2 * L], 0.0), strict, g)
    mk = each(lambda m_, x: jnp.where(m_, x[:L, 2 * L:], 0.0), strict, g)
    arb = each(lambda m_, x: jnp.where(m_, x[L:, :2 * L], 0.0), incl, g)
    ark = each(lambda m_, x: jnp.where(m_, x[L:, 2 * L:], 0.0), incl, g)
    tm = each(lambda e, lv, x: e + jnp.where(lv[0], x, 0.0), eye, levels, n)
    for li in range(1, len(levels[0])):
        x = each(lambda lv, n_, t: _bdot(jnp.where(lv[li], n_, 0.0), stack(t)), levels, n, tm)
        tm = each(lambda t, x_: t + _bdot(t, stack(x_)), tm, x)
    mkv = each(_bdot, mk, vv_s)
    wy = each(lambda t, ka, x: _bdot(t, jnp.concatenate([stack(ka), stack(x)], axis=1)), tm, kapt, mkv)
    z = each(_bdot, ark, vv_s)
    rest = each(lambda t, c: jnp.exp(t - c), tot, cum)
    gk = each(lambda v_, k_, e: _bdot_tn(v_, k_ * e), v, kd, rest)
    wrt = each(lambda x, rt_: jnp.concatenate([x[:, :LANES].astype(BF16), rt_], axis=0), wy, rt)
    y = each(lambda x: x[:, LANES:], wy)
    bp = each(lambda b_, e: (b_ * e).astype(BF16), bb, rest)
    return list(zip(wrt, y, arb, z, bp, gk, tot))


def _rwkv_chunk_states(s_prev, terms, consts):
    L = CHUNK
    each = lambda f, *cols: [f(*a) for a in zip(*cols)]
    wrt, y, arb, z, bp, gk, tot = (list(c) for c in zip(*terms))
    bd_full, m0, m1 = consts[3], consts[4], consts[5]
    stack = lambda x: jnp.concatenate([x * m0, x * m1], axis=0).astype(BF16)
    wr = each(_bdot_nt, wrt, s_prev)
    u = each(lambda x, y_: -(x[:L] + y_), wr, y)
    o = each(lambda x, a, u_, z_: x[L:] + _bdot(a, stack(u_)) + z_, wr, arb, u, z)
    s_new = each(lambda s, t, u_, b_, g_: jnp.where(bd_full, s * jnp.exp(t) + _bdot_tn(u_, b_) + g_, 0.0),
                 s_prev, tot, u, bp, gk)
    return s_new, o


def _rwkv_consts(reverse):
    L = CHUNK
    ri = lax.broadcasted_iota(jnp.int32, (L, 2 * L), 0)
    ci = lax.broadcasted_iota(jnp.int32, (L, 2 * L), 1) & (L - 1)
    strict = (ci > ri) if reverse else (ci < ri)
    incl = (ci >= ri) if reverse else (ci <= ri)
    r2 = lax.broadcasted_iota(jnp.int32, (L, L), 0)
    c2 = lax.broadcasted_iota(jnp.int32, (L, L), 1)
    tri_sel = jnp.where((c2 >= r2) if reverse else (c2 <= r2), 1.0, 0.0).astype(BF16)
    lane = lax.broadcasted_iota(jnp.int32, (1, LANES), 1)
    m0 = jnp.where(lane < HEAD, 1.0, 0.0)
    m1 = 1.0 - m0
    eye = jnp.where(ri == ci, 1.0, 0.0)
    rs = lax.broadcasted_iota(jnp.int32, (2 * HEAD, 2 * HEAD), 0)
    cs = lax.broadcasted_iota(jnp.int32, (2 * HEAD, 2 * HEAD), 1)
    bd_full = (rs >= HEAD) == (cs >= HEAD)
    levels = tuple(((ri >> (l + 1)) == (ci >> (l + 1))) & ((ri >> l) != (ci >> l))
                   for l in range(int(math.log2(L))))
    return (tri_sel, strict, incl, bd_full, m0, m1, eye, 0 if reverse else L - 1, levels)


def _rwkv_scan_kernel(*refs, npairs, nsub):
    ins = refs[:12]
    of_ref, ob_ref, s_ref = refs[12:]

    @pl.when(pl.program_id(2) == 0)
    def _():
        s_ref[...] = jnp.zeros(s_ref.shape, F32)

    consts = [_rwkv_consts(False), _rwkv_consts(True)]
    chains = [(dd, p) for dd in range(2) for p in range(npairs)]
    sub = lambda dd, k: (nsub - 1 - k) if dd == 1 else k
    jobs = []
    for k in range(nsub):
        for dd, p in chains:
            r_ref, v_ref, kap_ref, kd_ref, bb_ref, lw_ref = ins[6 * dd:6 * dd + 6]
            rs = slice(sub(dd, k) * CHUNK, (sub(dd, k) + 1) * CHUNK)
            sl = slice(p * LANES, (p + 1) * LANES)
            jobs.append((r_ref[rs, sl], v_ref[rs, sl], kap_ref[rs, sl], kd_ref[rs, sl], bb_ref[rs, sl],
                         lw_ref[rs, sl], consts[dd]))
    terms = _rwkv_chunk_terms(jobs)
    states = [s_ref[dd, p] for dd, p in chains]
    for k in range(nsub):
        states, outs = _rwkv_chunk_states(states, terms[k * len(chains):(k + 1) * len(chains)], consts[0])
        for (dd, p), o in zip(chains, outs):
            o_ref = ob_ref if dd == 1 else of_ref
            o_ref[sub(dd, k) * CHUNK:(sub(dd, k) + 1) * CHUNK, p * LANES:(p + 1) * LANES] = o
    for (dd, p), s_new in zip(chains, states):
        s_ref[dd, p] = s_new


def _rwkv_scan(r, v, kap, kd, bb, lw, geom):
    m, da = r.shape
    gw = _pick(da, (512, 256, 128))
    npairs = gw // LANES
    rows = _pick(geom.ctx, (4 * CHUNK, 2 * CHUNK, CHUNK))
    nsteps = (geom.seq + geom.ctx) // rows
    in_specs = []
    for dd in range(2):
        blk = lambda b, g, c, dd=dd: geom.chunk_block(b, c, rows, dd == 1)
        tok = pl.BlockSpec((rows, gw), lambda b, g, c, blk=blk: (blk(b, g, c), g))
        tokd = pl.BlockSpec((None, rows, gw), lambda b, g, c, blk=blk, dd=dd: (dd, blk(b, g, c), g))
        in_specs += [tok, tok, tok, tokd, tokd, tokd]
    out_specs = tuple(
        pl.BlockSpec((rows, gw), lambda b, g, c, dd=dd: (geom.chunk_block(b, c, rows, dd == 1), g))
        for dd in range(2))
    tok_shape = jax.ShapeDtypeStruct((m, da), F32)
    return pl.pallas_call(
        functools.partial(_rwkv_scan_kernel, npairs=npairs, nsub=rows // CHUNK),
        out_shape=(tok_shape, tok_shape),
        grid=(geom.batch, da // gw, nsteps),
        in_specs=in_specs,
        out_specs=out_specs,
        scratch_shapes=[pltpu.VMEM((2, npairs, LANES, LANES), F32)],
        compiler_params=_cparams(("parallel", "parallel", "arbitrary")),
        name="rwkv_scan",
    )(r, v, kap, kd, bb, lw, r, v, kap, kd, bb, lw)


def _rwkv_readout_kernel(of_ref, ob_ref, r_ref, v_ref, kd_ref, g_ref, rk_ref, gw_ref, gb_ref,
                         sel_ref, selt_ref, y_ref):
    sel, selt = sel_ref[...], selt_ref[...]
    head_mean = lambda x: _dot_sel_r(_dot_sel_r(x, sel) * (1.0 / HEAD), selt)
    o = of_ref[...] + ob_ref[...]
    cen = o - head_mean(o)
    var = head_mean(cen * cen)
    on = cen * lax.rsqrt(var + GN_EPS) * gw_ref[...] + gb_ref[...]
    bonus = _dot_sel_r(_dot_sel_r(r_ref[...] * (kd_ref[0] + kd_ref[1]) * rk_ref[...], sel), selt)
    y_ref[...] = ((on + bonus * v_ref[...]) * _silu(g_ref[...])).astype(y_ref.dtype)


def _rwkv_readout(o_f, o_b, r, v, kd, proj_b, r_k, gn_w, gn_b, sel, selt, tr):
    m, da = r.shape
    full = lambda a: pl.BlockSpec(a.shape, lambda i: (0,) * a.ndim)
    tok = pl.BlockSpec((tr, da), lambda i: (i, 0))
    params = (r_k, gn_w, gn_b, sel, selt)
    return pl.pallas_call(
        _rwkv_readout_kernel,
        out_shape=jax.ShapeDtypeStruct((m, da), BF16),
        grid=(m // tr,),
        in_specs=[tok, tok, tok, tok, pl.BlockSpec((2, tr, da), lambda i: (0, i, 0)), tok] + [full(p) for p in params],
        out_specs=tok,
        compiler_params=_cparams(("parallel",)),
        name="rwkv_readout",
    )(o_f, o_b, r, v, kd, proj_b, *params)


def _attn_rows(qs, k_loc, v_loc, bias, kc, vc, m0):
    each = lambda f, *cols: [f(*a) for a in zip(*cols)]
    n = qs[0].shape[0]
    q2 = each(lambda q: (jnp.concatenate([q * m0, q * (1.0 - m0)], axis=0) * (HEAD ** -0.5)).astype(BF16), qs)
    s_ctx = each(lambda q: _bdot_nt(q, kc), q2)
    mx = each(lambda s: jnp.max(s, axis=-1, keepdims=True), s_ctx)
    if k_loc is not None:
        s_loc = each(lambda q, k, b: _bdot_nt(q, k) + b, q2, k_loc, bias)
        mx = each(lambda m_, s: jnp.maximum(m_, jnp.max(s, axis=-1, keepdims=True)), mx, s_loc)
        p_loc = each(lambda s, m_: jnp.exp(s - m_), s_loc, mx)
    p_ctx = each(lambda s, m_: jnp.exp(s - m_), s_ctx, mx)
    den = each(lambda p: jnp.sum(p, axis=-1, keepdims=True), p_ctx)
    acc = each(lambda p: _bdot(p, vc), p_ctx)
    if k_loc is not None:
        den = each(lambda d_, p: d_ + jnp.sum(p, axis=-1, keepdims=True), den, p_loc)
        acc = each(lambda a, p, v: a + _bdot(p, v), acc, p_loc, v_loc)
    acc = each(lambda a, d_: a / d_, acc, den)
    return each(lambda a: jnp.where(m0 > 0.5, a[:n], a[n:]), acc)


def _na_kernel(q_ref, g_ref, k_ref, v_ref, kc_ref, vc_ref, bias_ref, y_ref, kb_ref, vb_ref, kcb_ref, vcb_ref,
               *, rows_per_step, img_rows):
    step = pl.program_id(2)
    lane = lax.broadcasted_iota(jnp.int32, (1, LANES), 1)
    m0 = jnp.where(lane < HEAD, 1.0, 0.0)
    kh = min(NA_WIN_H, img_rows)

    @pl.when(step == 0)
    def _():
        kb_ref[...] = k_ref[...].astype(BF16)
        vb_ref[...] = v_ref[...].astype(BF16)
        kcb_ref[...] = kc_ref[...].astype(BF16)
        vcb_ref[...] = vc_ref[...].astype(BF16)

    qs, k_loc, v_loc, bias = [], [], [], []
    for yy in range(rows_per_step):
        y = step * rows_per_step + yy
        y0 = jnp.clip(y - kh // 2, 0, img_rows - kh)
        start = pl.multiple_of(y0 * GRID_W, GRID_W)
        k_loc.append(kb_ref[pl.ds(start, kh * GRID_W), :])
        v_loc.append(vb_ref[pl.ds(start, kh * GRID_W), :])
        bias.append(bias_ref[:, y - y0].reshape(2 * GRID_W, kh * GRID_W))
        qs.append(q_ref[yy * GRID_W:(yy + 1) * GRID_W, :])
    outs = _attn_rows(qs, k_loc, v_loc, bias, kcb_ref[...], vcb_ref[...], m0)
    for yy, o in enumerate(outs):
        rs = slice(yy * GRID_W, (yy + 1) * GRID_W)
        y_ref[rs, :] = (o * _silu(g_ref[rs, :])).astype(y_ref.dtype)


def _ctx_attn_kernel(q_ref, g_ref, kc_ref, vc_ref, y_in_ref, y_ref):
    del y_in_ref
    lane = lax.broadcasted_iota(jnp.int32, (1, LANES), 1)
    m0 = jnp.where(lane < HEAD, 1.0, 0.0)
    o, = _attn_rows([q_ref[...]], None, None, None, kc_ref[...].astype(BF16), vc_ref[...].astype(BF16), m0)
    y_ref[...] = (o * _silu(g_ref[...])).astype(y_ref.dtype)


def _na_bias_table(rpb, img_rows):
    kh = min(NA_WIN_H, img_rows)
    cols = np.arange(GRID_W)
    col_start = np.clip(cols - NA_WIN_W // 2, 0, GRID_W - NA_WIN_W)
    kx = np.arange(GRID_W)
    inwin = (kx[None, :] >= col_start[:, None]) & (kx[None, :] < col_start[:, None] + NA_WIN_W)
    dx = kx[None, :] - cols[:, None] + (NA_WIN_W - 1)
    onehot = (np.arange(2 * NA_WIN_W - 1)[:, None, None] == dx[None]) & inwin[None]
    toep = jnp.einsum('hyd,dxk->hyxk', rpb, jnp.asarray(onehot, rpb.dtype), precision=lax.Precision.HIGHEST)
    toep = jnp.where(inwin[None, None], toep, NEG_BIG)
    tabs = [toep[:, NA_WIN_H - 1 - dl:NA_WIN_H - 1 - dl + kh] for dl in range(kh)]
    tab = jnp.stack(tabs, axis=1).transpose(0, 1, 3, 2, 4)
    return tab.reshape(rpb.shape[0], kh, GRID_W, kh * GRID_W)


def _na(proj_b, bias_tab, geom, db):
    m = proj_b.shape[0]
    npair = db // LANES
    img_rows = geom.seq // GRID_W
    rb = _pick(geom.seq, (8 * GRID_W, 4 * GRID_W, 2 * GRID_W, GRID_W))
    n_lat_steps = geom.seq // rb
    nblk = db // LANES
    ctx_blk = geom.n_lat // geom.ctx
    kh = min(NA_WIN_H, img_rows)
    y = pl.pallas_call(
        functools.partial(_na_kernel, rows_per_step=rb // GRID_W, img_rows=img_rows),
        out_shape=jax.ShapeDtypeStruct((m, db), BF16),
        grid=(geom.batch, npair, n_lat_steps),
        in_specs=[
            pl.BlockSpec((rb, LANES), lambda b, p, s: (b * n_lat_steps + s, 1 * nblk + p)),
            pl.BlockSpec((rb, LANES), lambda b, p, s: (b * n_lat_steps + s, 4 * nblk + p)),
            pl.BlockSpec((geom.seq, LANES), lambda b, p, s: (b, 2 * nblk + p)),
            pl.BlockSpec((geom.seq, LANES), lambda b, p, s: (b, 3 * nblk + p)),
            pl.BlockSpec((geom.ctx, LANES), lambda b, p, s: (ctx_blk + b, 2 * nblk + p)),
            pl.BlockSpec((geom.ctx, LANES), lambda b, p, s: (ctx_blk + b, 3 * nblk + p)),
            pl.BlockSpec((2, kh, GRID_W, kh * GRID_W), lambda b, p, s: (p, 0, 0, 0)),
        ],
        out_specs=pl.BlockSpec((rb, LANES), lambda b, p, s: (b * n_lat_steps + s, p)),
        scratch_shapes=[pltpu.VMEM((geom.seq, LANES), BF16), pltpu.VMEM((geom.seq, LANES), BF16),
                        pltpu.VMEM((geom.ctx, LANES), BF16), pltpu.VMEM((geom.ctx, LANES), BF16)],
        compiler_params=_cparams(("parallel", "parallel", "arbitrary")),
        name="nbr_attention",
    )(proj_b, proj_b, proj_b, proj_b, proj_b, proj_b, bias_tab)
    ctx_spec = lambda sec: pl.BlockSpec((geom.ctx, LANES), lambda b, p: (ctx_blk + b, sec * nblk + p))
    return pl.pallas_call(
        _ctx_attn_kernel,
        out_shape=jax.ShapeDtypeStruct((m, db), BF16),
        grid=(geom.batch, npair),
        in_specs=[ctx_spec(1), ctx_spec(4), ctx_spec(2), ctx_spec(3), pl.BlockSpec(memory_space=pl.ANY)],
        out_specs=pl.BlockSpec((geom.ctx, LANES), lambda b, p: (ctx_blk + b, p)),
        input_output_aliases={4: 0},
        compiler_params=_cparams(("parallel", "parallel")),
        name="ctx_attention",
    )(proj_b, proj_b, proj_b, proj_b, y)


def _rglru_kernel(*refs, geom, tc, gwc, reverse):
    if reverse:
        (x_ref, hp_ref, hn_ref, cw_ref, cb_ref, wa_ref, ba_ref, wx_ref, bx_ref, lam_ref, hf_ref, g_ref,
         out_ref, carry_ref, sa_ref, sb_ref, so_ref) = refs
    else:
        (x_ref, hp_ref, hn_ref, cw_ref, cb_ref, wa_ref, ba_ref, wx_ref, bx_ref, lam_ref,
         out_ref, carry_ref, sa_ref, sb_ref, so_ref) = refs
    b, c = pl.program_id(0), pl.program_id(2)

    @pl.when(c == 0)
    def _():
        carry_ref[...] = jnp.zeros(carry_ref.shape, F32)

    row0 = geom.chunk_block(b, c, tc, reverse) * tc
    is_first, is_last = geom.seq_edges(row0, tc)
    x = x_ref[...]
    nb = _shift_rows(x, hp_ref, hn_ref, slice(None), is_first, is_last, CONV_LEFT, CONV_W - 1 - CONV_LEFT)
    nb[0] = x
    u = cb_ref[...]
    for j in range(CONV_W):
        u = u + cw_ref[j:j + 1, :] * nb[j - CONV_LEFT]
    gate_r = _sigmoid(_bdot(u, wa_ref[0]) + ba_ref[...])
    gate_i = _sigmoid(_bdot(u, wx_ref[0]) + bx_ref[...])
    neg_lam = -lam_ref[...]
    softplus = jnp.maximum(neg_lam, 0.0) + jnp.log1p(jnp.exp(-jnp.abs(neg_lam)))
    log_a = (-RGLRU_C) * gate_r * softplus
    a = jnp.exp(log_a)
    th = jnp.tanh(log_a)
    bv = jnp.sqrt(-2.0 * th / (1.0 - th)) * (gate_i * u)
    for j in range(gwc // LANES):
        sa_ref[j] = a[:, j * LANES:(j + 1) * LANES]
        sb_ref[j] = bv[:, j * LANES:(j + 1) * LANES]
    span = SEG * SUBLANES
    r8 = lax.broadcasted_iota(jnp.int32, (SUBLANES, 1), 0)
    first, last = (SUBLANES - 1, 0) if reverse else (0, SUBLANES - 1)
    steps = list(reversed(range(SEG))) if reverse else list(range(SEG))
    for j in range(gwc // LANES):
        carry = carry_ref[0:1, j * LANES:(j + 1) * LANES]
        for q in (reversed(range(tc // span)) if reverse else range(tc // span)):
            rows = [pl.ds(q * span + i, SUBLANES, stride=SEG) for i in range(SEG)]
            av = [sa_ref[j, rw, :] for rw in rows]
            bw = [sb_ref[j, rw, :] for rw in rows]
            hs, ps = {}, {}
            h_run, p_run = bw[steps[0]], av[steps[0]]
            hs[steps[0]], ps[steps[0]] = h_run, p_run
            for i in steps[1:]:
                h_run = av[i] * h_run + bw[i]
                p_run = av[i] * p_run
                hs[i], ps[i] = h_run, p_run
            pp, hh = p_run, h_run
            s_ = 1
            while s_ < SUBLANES:
                if reverse:
                    p_s, h_s = pltpu.roll(pp, SUBLANES - s_, 0), pltpu.roll(hh, SUBLANES - s_, 0)
                    valid = r8 < SUBLANES - s_
                else:
                    p_s, h_s = pltpu.roll(pp, s_, 0), pltpu.roll(hh, s_, 0)
                    valid = r8 >= s_
                hh = jnp.where(valid, pp * h_s + hh, hh)
                pp = jnp.where(valid, pp * p_s, pp)
                s_ *= 2
            state = hh + pp * carry
            shifted = pltpu.roll(state, (SUBLANES - 1) if reverse else 1, 0)
            cin = jnp.where(r8 == first, carry, shifted)
            carry = state[last:last + 1, :]
            for i in range(SEG):
                so_ref[j, rows[i], :] = hs[i] + ps[i] * cin
        carry_ref[:, j * LANES:(j + 1) * LANES] = jnp.broadcast_to(carry, (SUBLANES, LANES))
    h = jnp.concatenate([so_ref[j] for j in range(gwc // LANES)], axis=1)
    if reverse:
        out_ref[...] = ((hf_ref[...] + h) * _silu(g_ref[...])).astype(out_ref.dtype)
    else:
        out_ref[...] = h


def _rglru(proj_c, geom, tc, gwc, conv_w, conv_b, wa_g, ba, wx_g, bx, lam, reverse, h_f=None):
    m = proj_c.shape[0]
    dc = conv_w.shape[1]
    ngrp = dc // gwc
    nch = (geom.seq + geom.ctx) // tc
    nb8 = m // SUBLANES
    r8 = tc // SUBLANES
    blk = lambda b, g, c: geom.chunk_block(b, c, tc, reverse)
    vec = lambda rows: pl.BlockSpec((rows, gwc), lambda b, g, c: (0, g))
    in_specs = [
        pl.BlockSpec((tc, gwc), lambda b, g, c: (blk(b, g, c), g)),
        pl.BlockSpec((SUBLANES, gwc), lambda b, g, c: (jnp.maximum(blk(b, g, c) * r8 - 1, 0), g)),
        pl.BlockSpec((SUBLANES, gwc), lambda b, g, c: (jnp.minimum((blk(b, g, c) + 1) * r8, nb8 - 1), g)),
        vec(CONV_W), vec(1),
        pl.BlockSpec((1, gwc, gwc), lambda b, g, c: (g, 0, 0)), vec(1),
        pl.BlockSpec((1, gwc, gwc), lambda b, g, c: (g, 0, 0)), vec(1),
        vec(1),
    ]
    args = [proj_c, proj_c, proj_c, conv_w, conv_b, wa_g, ba, wx_g, bx, lam]
    if reverse:
        in_specs += [
            pl.BlockSpec((tc, gwc), lambda b, g, c: (blk(b, g, c), g)),
            pl.BlockSpec((tc, gwc), lambda b, g, c: (blk(b, g, c), ngrp + g)),
        ]
        args += [h_f, proj_c]
    return pl.pallas_call(
        functools.partial(_rglru_kernel, geom=geom, tc=tc, gwc=gwc, reverse=reverse),
        out_shape=jax.ShapeDtypeStruct((m, dc), BF16 if reverse else F32),
        grid=(geom.batch, ngrp, nch),
        in_specs=in_specs,
        out_specs=pl.BlockSpec((tc, gwc), lambda b, g, c: (blk(b, g, c), g)),
        scratch_shapes=[pltpu.VMEM((SUBLANES, gwc), F32)] + [pltpu.VMEM((gwc // LANES, tc, LANES), F32)] * 3,
        compiler_params=_cparams(("parallel", "parallel", "arbitrary")),
        name="rglru_bwd" if reverse else "rglru_fwd",
    )(*args)


def _group_blockdiag(w, gwc):
    nblk, bs, _ = w.shape
    per = gwc // bs
    wg = w.reshape(nblk // per, per, bs, bs)
    eye = jnp.eye(per, dtype=w.dtype)
    dense = jnp.einsum('gpij,pq->gpiqj', wg, eye)
    return dense.reshape(nblk // per, gwc, gwc)


def kernel(x, c, ctx, c_ctx, mod_w, mod_b, norm_pre, norm_post, ev_w_in, ev_mu, ev_w0, ev_w_up, ev_a0, ev_a_up, ev_k_k, ev_k_a, ev_r_k, ev_gn_w, ev_gn_b, ev_rpb, ev_w_out, od_w_in, od_conv_w, od_conv_b, od_gate_a_w, od_gate_a_b, od_gate_x_w, od_gate_x_b, od_lambda, od_w_out):
    batch, seq, d = x.shape
    n_ctx = ctx.shape[1]
    depth = mod_w.shape[0]
    da = ev_k_k.shape[1]
    rw = ev_w_up.shape[2]
    db = ev_w_out.shape[1] - da
    dc = od_conv_w.shape[2]
    rg_bs = od_gate_a_w.shape[-1]
    assert da % LANES == 0 and db == da and 2 * rw == LANES and ev_a_up.shape[2] == rw
    assert batch < SUBLANES and seq % GRID_W == 0 and seq % n_ctx == 0 and n_ctx % CHUNK == 0
    assert n_ctx % GRID_W == 0 and (batch * seq) % n_ctx == 0
    geom = _Geom(batch, seq, n_ctx)
    m, n_lat = geom.m, geom.n_lat

    tm = _pick(math.gcd(n_lat, batch * n_ctx), (512, 256, 128, 64))
    tp = _pick(math.gcd(seq, n_ctx), (256, 128, 64))
    lat_blocks = n_lat // tm
    per_batch = seq // tm
    mod_row = lambda i: jnp.where(i < lat_blocks, i // per_batch, batch)

    c8 = jnp.zeros((SUBLANES, d), F32).at[:batch].set(c).at[batch].set(c_ctx)
    mods = _modulation(c8, mod_w, mod_b).reshape(depth, SUBLANES, 1, 3 * d)

    x_all = jnp.concatenate([x.reshape(n_lat, d), ctx.reshape(batch * n_ctx, d)], axis=0)

    heads_a = da // HEAD
    sel_np = np.zeros((da, LANES), np.float32)
    sel_np[np.arange(da), np.arange(da) // HEAD] = 1.0
    sel = jnp.asarray(sel_np, BF16)
    selt = jnp.asarray(sel_np.T, BF16)
    assert heads_a <= LANES

    a_shifted = 3 * da + 4 * rw
    gwc = rg_bs * LANES // math.gcd(rg_bs, LANES)

    for layer in range(depth):
        last = layer == depth - 1
        i = layer // 2
        mods_l = mods[layer]
        g_pre = norm_pre[layer].reshape(1, d)
        g_post = norm_post[layer].reshape(1, d)
        m_out = n_lat if last else m
        h = _normmod(x_all, mods_l, g_pre, tm, mod_row)
        if layer % 2 == 0:
            w_in = ev_w_in[i].astype(BF16)
            proj_a = _inproj(h, w_in[:, :a_shifted])
            proj_b = _inproj(h, w_in[:, a_shifted:])
            pad_dir = lambda w: jnp.stack([
                jnp.concatenate([w[0], jnp.zeros_like(w[1])], axis=0),
                jnp.concatenate([jnp.zeros_like(w[0]), w[1]], axis=0)]).astype(BF16)
            r, v, kap, kd, bb, lw = _rwkv_prep(
                proj_a, geom, tp, ev_mu[i], ev_w0[i], pad_dir(ev_w_up[i]), ev_a0[i], pad_dir(ev_a_up[i]),
                ev_k_k[i].reshape(1, da), ev_k_a[i].reshape(1, da), sel, selt)
            o_f, o_b = _rwkv_scan(r, v, kap, kd, bb, lw, geom)
            y_a = _rwkv_readout(o_f, o_b, r, v, kd, proj_b, ev_r_k[i].reshape(1, da),
                                ev_gn_w[i].reshape(1, da), ev_gn_b[i].reshape(1, da), sel, selt, tp)
            y_b = _na(proj_b, _na_bias_table(ev_rpb[i], seq // GRID_W), geom, db)
            w_out = ev_w_out[i].astype(BF16)
            x_all = _outproj([y_a, y_b], [w_out[:da], w_out[da:]], x_all, mods_l, g_post, tm, mod_row, m_out)
        else:
            proj_c = _inproj(h, od_w_in[i].astype(BF16))
            row = lambda a: a.reshape(1, dc)
            h_f = None
            for dd in range(2):
                out = _rglru(proj_c, geom, tp, gwc, od_conv_w[i], row(od_conv_b[i]),
                             _group_blockdiag(od_gate_a_w[i, dd], gwc).astype(BF16), row(od_gate_a_b[i, dd]),
                             _group_blockdiag(od_gate_x_w[i, dd], gwc).astype(BF16), row(od_gate_x_b[i, dd]),
                             row(od_lambda[i, dd]), dd == 1, h_f)
                h_f = out
            x_all = _outproj([out], [od_w_out[i].astype(BF16)], x_all, mods_l, g_post, tm, mod_row, m_out)
    return x_all.reshape(batch, seq, d)
```

```python
import functools
import math

import jax
import jax.numpy as jnp
import numpy as np
from jax import lax
from jax.experimental import pallas as pl
from jax.experimental.pallas import tpu as pltpu

F32 = jnp.float32
BF16 = jnp.bfloat16

EPS_RMS = 1e-6
GN_EPS = 64e-5
RGLRU_C = 8.0
HEAD = 64
GRID_W = 64
NA_WIN_H = 8
NA_WIN_W = 16
CONV_W = 4
CONV_LEFT = 2
LANES = 128
SUBLANES = 8
SEG = 4
CHUNK = 64
NEG_BIG = -1e30
VMEM_LIMIT = 56 * 1024 * 1024


def _cparams(sem):
    return pltpu.CompilerParams(dimension_semantics=sem, vmem_limit_bytes=VMEM_LIMIT)


def _pick(n, cands):
    for c in cands:
        if n % c == 0:
            return c
    raise ValueError(f"no tile for {n} in {cands}")


def _bdot(a, b):
    return jnp.dot(a.astype(BF16), b.astype(BF16), preferred_element_type=F32)


def _bdot_nt(a, b):
    return lax.dot_general(a.astype(BF16), b.astype(BF16), (((1,), (1,)), ((), ())),
                           preferred_element_type=F32)


def _bdot_tn(a, b):
    return lax.dot_general(a.astype(BF16), b.astype(BF16), (((0,), (0,)), ((), ())),
                           preferred_element_type=F32)


def _split3(x):
    hi = x.astype(BF16)
    r1 = x - hi.astype(F32)
    mid = r1.astype(BF16)
    lo = (r1 - mid.astype(F32)).astype(BF16)
    return hi, mid, lo


def _dot_sel_r(x, sel):
    hi, mid, lo = _split3(x)
    d = lambda t: jnp.dot(t, sel, preferred_element_type=F32)
    return d(hi) + d(mid) + d(lo)


def _cumsum_rows(x, reverse):
    n = x.shape[0] // SUBLANES
    x3 = x.reshape(n, SUBLANES, x.shape[1])
    r8 = lax.broadcasted_iota(jnp.int32, (1, SUBLANES, 1), 1)
    s = 1
    while s < SUBLANES:
        if reverse:
            x3 = x3 + jnp.where(r8 < SUBLANES - s, pltpu.roll(x3, SUBLANES - s, 1), 0.0)
        else:
            x3 = x3 + jnp.where(r8 >= s, pltpu.roll(x3, s, 1), 0.0)
        s *= 2
    edge = 0 if reverse else SUBLANES - 1
    groups, run = [None] * n, None
    for gi in (reversed(range(n)) if reverse else range(n)):
        groups[gi] = x3[gi] if run is None else x3[gi] + run
        run = groups[gi][edge:edge + 1, :]
    return jnp.concatenate(groups, axis=0)


def _sigmoid(x):
    return 1.0 / (1.0 + jnp.exp(-x))


def _silu(x):
    return x * _sigmoid(x)


def _mod_kernel(c_ref, w_ref, b_ref, o_ref):
    cs = _silu(c_ref[...])
    o_ref[...] = _bdot(cs, w_ref[...]) + b_ref[...]


def _modulation(c8, mod_w, mod_b):
    depth, d, n = mod_w.shape
    tn = _pick(n, (1024, 512, 256, 128))
    return pl.pallas_call(
        _mod_kernel,
        out_shape=jax.ShapeDtypeStruct((depth, SUBLANES, n), F32),
        grid=(depth, n // tn),
        in_specs=[
            pl.BlockSpec((SUBLANES, d), lambda l, j: (0, 0)),
            pl.BlockSpec((None, d, tn), lambda l, j: (l, 0, j)),
            pl.BlockSpec((None, 1, tn), lambda l, j: (l, 0, j)),
        ],
        out_specs=pl.BlockSpec((None, SUBLANES, tn), lambda l, j: (l, 0, j)),
        compiler_params=_cparams(("parallel", "parallel")),
        name="modulation",
    )(c8, mod_w, mod_b.reshape(depth, 1, n))


def _normmod_kernel(x_ref, mod_ref, g_ref, h_ref, *, d):
    x = x_ref[...]
    ms = jnp.mean(x * x, axis=-1, keepdims=True)
    y = x * lax.rsqrt(ms + EPS_RMS) * g_ref[...]
    h_ref[...] = (y * (1.0 + mod_ref[:, d:2 * d]) + mod_ref[:, 0:d]).astype(h_ref.dtype)


def _normmod(x_all, mods_l, g, tm, mod_row):
    m, d = x_all.shape
    return pl.pallas_call(
        functools.partial(_normmod_kernel, d=d),
        out_shape=jax.ShapeDtypeStruct((m, d), BF16),
        grid=(m // tm,),
        in_specs=[
            pl.BlockSpec((tm, d), lambda i: (i, 0)),
            pl.BlockSpec((None, 1, 3 * d), lambda i: (mod_row(i), 0, 0)),
            pl.BlockSpec((1, d), lambda i: (0, 0)),
        ],
        out_specs=pl.BlockSpec((tm, d), lambda i: (i, 0)),
        compiler_params=_cparams(("parallel",)),
        name="normmod",
    )(x_all, mods_l, g)


def _inproj_kernel(h_ref, w_ref, o_ref):
    o_ref[...] = jnp.dot(h_ref[...], w_ref[...], preferred_element_type=F32)


def _inproj(h, w_bf):
    m, d = h.shape
    n = w_bf.shape[1]
    tm = _pick(m, (1024, 512, 256, 128, 64))
    tn = _pick(n, (1664, 1408, 1280, 1024, 768, 640, 512, 384, 256, 128))
    return pl.pallas_call(
        _inproj_kernel,
        out_shape=jax.ShapeDtypeStruct((m, n), F32),
        grid=(n // tn, m // tm),
        in_specs=[
            pl.BlockSpec((tm, d), lambda j, i: (i, 0)),
            pl.BlockSpec((d, tn), lambda j, i: (0, j)),
        ],
        out_specs=pl.BlockSpec((tm, tn), lambda j, i: (i, j)),
        compiler_params=_cparams(("parallel", "parallel")),
        name="inproj",
    )(h, w_bf)


def _outproj_kernel(*refs, n_in, d):
    y_refs = refs[:n_in]
    w_refs = refs[n_in:2 * n_in]
    x_ref, mod_ref, g_ref, o_ref = refs[2 * n_in:]
    acc = jnp.dot(y_refs[0][...], w_refs[0][...], preferred_element_type=F32)
    for i in range(1, n_in):
        acc = acc + jnp.dot(y_refs[i][...], w_refs[i][...], preferred_element_type=F32)
    ms = jnp.mean(acc * acc, axis=-1, keepdims=True)
    z = acc * lax.rsqrt(ms + EPS_RMS) * g_ref[...]
    o_ref[...] = x_ref[...] + mod_ref[:, 2 * d:3 * d] * z


def _outproj(ys, ws, x_all, mods_l, g, tm, mod_row, m_out):
    d = x_all.shape[1]
    n_in = len(ys)
    in_specs = [pl.BlockSpec((tm, y.shape[1]), lambda i: (i, 0)) for y in ys]
    in_specs += [pl.BlockSpec(w.shape, lambda i: (0, 0)) for w in ws]
    in_specs += [
        pl.BlockSpec((tm, d), lambda i: (i, 0)),
        pl.BlockSpec((None, 1, 3 * d), lambda i: (mod_row(i), 0, 0)),
        pl.BlockSpec((1, d), lambda i: (0, 0)),
    ]
    return pl.pallas_call(
        functools.partial(_outproj_kernel, n_in=n_in, d=d),
        out_shape=jax.ShapeDtypeStruct((m_out, d), F32),
        grid=(m_out // tm,),
        in_specs=in_specs,
        out_specs=pl.BlockSpec((tm, d), lambda i: (i, 0)),
        compiler_params=_cparams(("parallel",)),
        name="outproj",
    )(*ys, *ws, x_all, mods_l, g)


class _Geom:
    def __init__(self, batch, seq, ctx):
        self.batch, self.seq, self.ctx = batch, seq, ctx
        self.n_lat = batch * seq
        self.m = batch * (seq + ctx)

    def seq_edges(self, row0, rows):
        in_lat = row0 < self.n_lat
        off = jnp.where(in_lat, row0 % self.seq, (row0 - self.n_lat) % self.ctx)
        length = jnp.where(in_lat, self.seq, self.ctx)
        return off == 0, off + rows == length

    def chunk_block(self, b, c, rows, reverse):
        cc = self.ctx // rows
        sc = self.seq // rows
        if reverse:
            ctx_blk = self.n_lat // rows + b * cc + (cc - 1 - c)
            lat_blk = b * sc + (sc - 1 - (c - cc))
        else:
            ctx_blk = self.n_lat // rows + b * cc + c
            lat_blk = b * sc + (c - cc)
        return jnp.where(c < cc, ctx_blk, lat_blk)


def _shift_rows(x, hp_ref, hn_ref, cols, is_first, is_last, back, fwd):
    rows = x.shape[0]
    r8 = lax.broadcasted_iota(jnp.int32, (SUBLANES, 1), 0)
    hp = jnp.where(is_first, 0.0, hp_ref[:, cols])
    hn = jnp.where(is_last, 0.0, hn_ref[:, cols])
    outs = {}
    for s in range(1, back + 1):
        y = pltpu.roll(x, s, 0)
        head = jnp.where(r8 < s, pltpu.roll(hp, s, 0), y[:SUBLANES])
        outs[-s] = jnp.concatenate([head, y[SUBLANES:]], axis=0)
    for s in range(1, fwd + 1):
        y = pltpu.roll(x, rows - s, 0)
        tail = jnp.where(r8 >= SUBLANES - s, pltpu.roll(hn, SUBLANES - s, 0), y[rows - SUBLANES:])
        outs[s] = jnp.concatenate([y[:rows - SUBLANES], tail], axis=0)
    return outs


def _halo_specs(tp, width, m):
    nb8 = m // SUBLANES
    r8 = tp // SUBLANES
    return [
        pl.BlockSpec((SUBLANES, width), lambda i: (jnp.maximum(i * r8 - 1, 0), 0)),
        pl.BlockSpec((SUBLANES, width), lambda i: (jnp.minimum((i + 1) * r8, nb8 - 1), 0)),
    ]


def _rwkv_prep_kernel(f_ref, hp_ref, hn_ref, mu_ref, w0_ref, wup_ref, a0_ref, aup_ref, kk_ref, ka_ref,
                      sel_ref, selt_ref, r_ref, v_ref, kap_ref, kd_ref, bb_ref, lw_ref, *, geom, tp, da):
    i = pl.program_id(0)
    is_first, is_last = geom.seq_edges(i * tp, tp)

    def shifted(lo, hi):
        cols = slice(lo, hi)
        x = f_ref[:, cols]
        nb = _shift_rows(x, hp_ref, hn_ref, cols, is_first, is_last, 1, 1)
        return x + mu_ref[0:1, cols] * (nb[-1] - x) + mu_ref[1:2, cols] * (nb[1] - x)

    r = shifted(0, da)
    k = shifted(da, 2 * da)
    v = shifted(2 * da, 3 * da)
    cw = jnp.tanh(shifted(3 * da, 3 * da + LANES))
    ca = shifted(3 * da + LANES, 3 * da + 2 * LANES)
    r_ref[...] = r.astype(r_ref.dtype)
    v_ref[...] = v.astype(v_ref.dtype)
    kk = k * kk_ref[...]
    ss = _dot_sel_r(kk * kk, sel_ref[...])
    inv = lax.rsqrt(ss + 1e-12)
    kap = kk * _dot_sel_r(inv, selt_ref[...])
    kap_ref[...] = kap.astype(kap_ref.dtype)
    for dd in range(2):
        zw = w0_ref[dd:dd + 1, :] + _bdot(cw, wup_ref[dd])
        lw_ref[dd] = -math.exp(-0.5) * _sigmoid(zw)
        a = _sigmoid(a0_ref[dd:dd + 1, :] + _bdot(ca, aup_ref[dd]))
        kd_ref[dd] = (k * (1.0 + (a - 1.0) * ka_ref[...])).astype(kd_ref.dtype)
        bb_ref[dd] = (kap * a).astype(bb_ref.dtype)


def _rwkv_prep(proj_a, geom, tp, mu, w0, wup_pad, a0, aup_pad, k_k, k_a, sel, selt):
    m, wa = proj_a.shape
    da = k_k.shape[1]
    full = lambda a: pl.BlockSpec(a.shape, lambda i: (0,) * a.ndim)
    tok = jax.ShapeDtypeStruct((m, da), BF16)
    tok2 = jax.ShapeDtypeStruct((2, m, da), BF16)
    params = (mu, w0, wup_pad, a0, aup_pad, k_k, k_a, sel, selt)
    return pl.pallas_call(
        functools.partial(_rwkv_prep_kernel, geom=geom, tp=tp, da=da),
        out_shape=(tok, tok, tok, tok2, tok2, jax.ShapeDtypeStruct((2, m, da), F32)),
        grid=(m // tp,),
        in_specs=[pl.BlockSpec((tp, wa), lambda i: (i, 0))] + _halo_specs(tp, wa, m) + [full(p) for p in params],
        out_specs=(
            pl.BlockSpec((tp, da), lambda i: (i, 0)),
            pl.BlockSpec((tp, da), lambda i: (i, 0)),
            pl.BlockSpec((tp, da), lambda i: (i, 0)),
            pl.BlockSpec((2, tp, da), lambda i: (0, i, 0)),
            pl.BlockSpec((2, tp, da), lambda i: (0, i, 0)),
            pl.BlockSpec((2, tp, da), lambda i: (0, i, 0)),
        ),
        compiler_params=_cparams(("parallel",)),
        name="rwkv_prep",
    )(proj_a, proj_a, proj_a, *params)


def _rwkv_chunk_terms(jobs):
    L = CHUNK
    each = lambda f, *cols: [f(*a) for a in zip(*cols)]
    r, v, kap, kd, bb, lw, consts = (list(c) for c in zip(*jobs))
    reverse, strict, incl, bd_full, m0, m1, eye, last_row, levels = (list(c) for c in zip(*consts))
    stack = lambda x: jnp.concatenate([x * m0[0], x * m1[0]], axis=0).astype(BF16)

    cum = each(_cumsum_rows, lw, reverse)
    tot = each(lambda c, lr: c[lr:lr + 1, :], cum, last_row)
    rt = each(lambda x, c: (x * jnp.exp(c)).astype(BF16), r, cum)
    e_neg = each(lambda c: jnp.exp(-c), cum)
    kt = each(lambda x, e: x * e, kd, e_neg)
    bt = each(lambda x, e: x * e, bb, e_neg)
    kapt = each(lambda x, c, l: (x * jnp.exp(c - l)).astype(BF16), kap, cum, lw)
    vv_s = each(stack, v)
    g = each(lambda ka, x, b_, k_: _bdot_nt(jnp.concatenate([ka, x], axis=0),
                                            jnp.concatenate([stack(b_), stack(k_)], axis=0)),
             kapt, rt, bt, kt)
    n = each(lambda m_, x: jnp.where(m_, -x[:L, :2 * L], 0.0), strict, g)
    mk = each(lambda m_, x: jnp.where(m_, x[:L, 2 * L:], 0.0), strict, g)
    arb = each(lambda m_, x: jnp.where(m_, x[L:, :2 * L], 0.0), incl, g)
    ark = each(lambda m_, x: jnp.where(m_, x[L:, 2 * L:], 0.0), incl, g)
    mkz = each(lambda m_, a_, vv: _bdot(jnp.concatenate([m_, a_], axis=0), vv), mk, ark, vv_s)
    tm = each(lambda e, lv, x: e + jnp.where(lv[0], x, 0.0), eye, levels, n)
    for li in range(1, len(levels[0])):
        x = each(lambda lv, n_, t: _bdot(jnp.where(lv[li], n_, 0.0), stack(t)), levels, n, tm)
        tm = each(lambda t, x_: t + _bdot(t, stack(x_)), tm, x)
    wy = each(lambda t, ka, x: _bdot(t, jnp.concatenate([stack(ka), stack(x[:L])], axis=1)), tm, kapt, mkz)
    z = each(lambda x: x[L:], mkz)
    rest = each(lambda t, c: jnp.exp(t - c), tot, cum)
    gk = each(lambda v_, k_, e: _bdot_tn(v_, k_ * e), v, kd, rest)
    wrt = each(lambda x, rt_: jnp.concatenate([x[:, :LANES].astype(BF16), rt_], axis=0), wy, rt)
    y = each(lambda x: x[:, LANES:], wy)
    bp = each(lambda b_, e: (b_ * e).astype(BF16), bb, rest)
    return list(zip(wrt, y, arb, z, bp, gk, tot))


def _rwkv_chunk_states(s_prev, terms, consts):
    L = CHUNK
    each = lambda f, *cols: [f(*a) for a in zip(*cols)]
    wrt, y, arb, z, bp, gk, tot = (list(c) for c in zip(*terms))
    bd_full, m0, m1 = consts[3], consts[4], consts[5]
    stack = lambda x: jnp.concatenate([x * m0, x * m1], axis=0).astype(BF16)
    wr = each(_bdot_nt, wrt, s_prev)
    u = each(lambda x, y_: -(x[:L] + y_), wr, y)
    o = each(lambda x, a, u_, z_: x[L:] + _bdot(a, stack(u_)) + z_, wr, arb, u, z)
    s_new = each(lambda s, t, u_, b_, g_: jnp.where(bd_full, s * jnp.exp(t) + _bdot_tn(u_, b_) + g_, 0.0),
                 s_prev, tot, u, bp, gk)
    return s_new, o


def _rwkv_consts(reverse):
    L = CHUNK
    ri = lax.broadcasted_iota(jnp.int32, (L, 2 * L), 0)
    ci = lax.broadcasted_iota(jnp.int32, (L, 2 * L), 1) & (L - 1)
    strict = (ci > ri) if reverse else (ci < ri)
    incl = (ci >= ri) if reverse else (ci <= ri)
    lane = lax.broadcasted_iota(jnp.int32, (1, LANES), 1)
    m0 = jnp.where(lane < HEAD, 1.0, 0.0)
    m1 = 1.0 - m0
    eye = jnp.where(ri == ci, 1.0, 0.0)
    rs = lax.broadcasted_iota(jnp.int32, (2 * HEAD, 2 * HEAD), 0)
    cs = lax.broadcasted_iota(jnp.int32, (2 * HEAD, 2 * HEAD), 1)
    bd_full = (rs >= HEAD) == (cs >= HEAD)
    levels = tuple(((ri >> (l + 1)) == (ci >> (l + 1))) & ((ri >> l) != (ci >> l))
                   for l in range(int(math.log2(L))))
    return (reverse, strict, incl, bd_full, m0, m1, eye, 0 if reverse else L - 1, levels)


def _rwkv_scan_kernel(*refs, npairs, nsub):
    ins = refs[:12]
    of_ref, ob_ref, s_ref = refs[12:]

    @pl.when(pl.program_id(2) == 0)
    def _():
        s_ref[...] = jnp.zeros(s_ref.shape, F32)

    consts = [_rwkv_consts(False), _rwkv_consts(True)]
    chains = [(dd, p) for dd in range(2) for p in range(npairs)]
    sub = lambda dd, k: (nsub - 1 - k) if dd == 1 else k
    jobs = []
    for k in range(nsub):
        for dd, p in chains:
            r_ref, v_ref, kap_ref, kd_ref, bb_ref, lw_ref = ins[6 * dd:6 * dd + 6]
            rs = slice(sub(dd, k) * CHUNK, (sub(dd, k) + 1) * CHUNK)
            sl = slice(p * LANES, (p + 1) * LANES)
            jobs.append((r_ref[rs, sl], v_ref[rs, sl], kap_ref[rs, sl], kd_ref[rs, sl], bb_ref[rs, sl],
                         lw_ref[rs, sl], consts[dd]))
    terms = _rwkv_chunk_terms(jobs)
    states = [s_ref[dd, p] for dd, p in chains]
    for k in range(nsub):
        states, outs = _rwkv_chunk_states(states, terms[k * len(chains):(k + 1) * len(chains)], consts[0])
        for (dd, p), o in zip(chains, outs):
            o_ref = ob_ref if dd == 1 else of_ref
            o_ref[sub(dd, k) * CHUNK:(sub(dd, k) + 1) * CHUNK, p * LANES:(p + 1) * LANES] = o
    for (dd, p), s_new in zip(chains, states):
        s_ref[dd, p] = s_new


def _rwkv_scan(r, v, kap, kd, bb, lw, geom):
    m, da = r.shape
    gw = _pick(da, (1024, 512, 256, 128))
    npairs = gw // LANES
    rows = _pick(geom.ctx, (4 * CHUNK, 2 * CHUNK, CHUNK))
    nsteps = (geom.seq + geom.ctx) // rows
    in_specs = []
    for dd in range(2):
        blk = lambda b, g, c, dd=dd: geom.chunk_block(b, c, rows, dd == 1)
        tok = pl.BlockSpec((rows, gw), lambda b, g, c, blk=blk: (blk(b, g, c), g))
        tokd = pl.BlockSpec((None, rows, gw), lambda b, g, c, blk=blk, dd=dd: (dd, blk(b, g, c), g))
        in_specs += [tok, tok, tok, tokd, tokd, tokd]
    out_specs = tuple(
        pl.BlockSpec((rows, gw), lambda b, g, c, dd=dd: (geom.chunk_block(b, c, rows, dd == 1), g))
        for dd in range(2))
    tok_shape = jax.ShapeDtypeStruct((m, da), F32)
    return pl.pallas_call(
        functools.partial(_rwkv_scan_kernel, npairs=npairs, nsub=rows // CHUNK),
        out_shape=(tok_shape, tok_shape),
        grid=(geom.batch, da // gw, nsteps),
        in_specs=in_specs,
        out_specs=out_specs,
        scratch_shapes=[pltpu.VMEM((2, npairs, LANES, LANES), F32)],
        compiler_params=_cparams(("parallel", "parallel", "arbitrary")),
        name="rwkv_scan",
    )(r, v, kap, kd, bb, lw, r, v, kap, kd, bb, lw)


def _rwkv_readout_kernel(of_ref, ob_ref, r_ref, v_ref, kd_ref, g_ref, rk_ref, gw_ref, gb_ref,
                         sel_ref, selt_ref, y_ref):
    sel, selt = sel_ref[...], selt_ref[...]
    head_mean = lambda x: _dot_sel_r(_dot_sel_r(x, sel) * (1.0 / HEAD), selt)
    o = of_ref[...] + ob_ref[...]
    cen = o - head_mean(o)
    var = head_mean(cen * cen)
    on = cen * lax.rsqrt(var + GN_EPS) * gw_ref[...] + gb_ref[...]
    bonus = _dot_sel_r(_dot_sel_r(r_ref[...] * (kd_ref[0] + kd_ref[1]) * rk_ref[...], sel), selt)
    y_ref[...] = ((on + bonus * v_ref[...]) * _silu(g_ref[...])).astype(y_ref.dtype)


def _rwkv_readout(o_f, o_b, r, v, kd, proj_b, r_k, gn_w, gn_b, sel, selt, tr):
    m, da = r.shape
    full = lambda a: pl.BlockSpec(a.shape, lambda i: (0,) * a.ndim)
    tok = pl.BlockSpec((tr, da), lambda i: (i, 0))
    params = (r_k, gn_w, gn_b, sel, selt)
    return pl.pallas_call(
        _rwkv_readout_kernel,
        out_shape=jax.ShapeDtypeStruct((m, da), BF16),
        grid=(m // tr,),
        in_specs=[tok, tok, tok, tok, pl.BlockSpec((2, tr, da), lambda i: (0, i, 0)), tok] + [full(p) for p in params],
        out_specs=tok,
        compiler_params=_cparams(("parallel",)),
        name="rwkv_readout",
    )(o_f, o_b, r, v, kd, proj_b, *params)


def _attn_rows(qs, k_loc, v_loc, bias, kc, vc, m0):
    each = lambda f, *cols: [f(*a) for a in zip(*cols)]
    n = qs[0].shape[0]
    q2 = each(lambda q: (jnp.concatenate([q * m0, q * (1.0 - m0)], axis=0) * (HEAD ** -0.5)).astype(BF16), qs)
    s_ctx = each(lambda q: _bdot_nt(q, kc), q2)
    mx = each(lambda s: jnp.max(s, axis=-1, keepdims=True), s_ctx)
    if k_loc is not None:
        s_loc = each(lambda q, k, b: _bdot_nt(q, k) + b, q2, k_loc, bias)
        mx = each(lambda m_, s: jnp.maximum(m_, jnp.max(s, axis=-1, keepdims=True)), mx, s_loc)
        p_loc = each(lambda s, m_: jnp.exp(s - m_), s_loc, mx)
    p_ctx = each(lambda s, m_: jnp.exp(s - m_), s_ctx, mx)
    den = each(lambda p: jnp.sum(p, axis=-1, keepdims=True), p_ctx)
    acc = each(lambda p: _bdot(p, vc), p_ctx)
    if k_loc is not None:
        den = each(lambda d_, p: d_ + jnp.sum(p, axis=-1, keepdims=True), den, p_loc)
        acc = each(lambda a, p, v: a + _bdot(p, v), acc, p_loc, v_loc)
    acc = each(lambda a, d_: a / d_, acc, den)
    return each(lambda a: jnp.where(m0 > 0.5, a[:n], a[n:]), acc)


def _na_kernel(q_ref, g_ref, k_ref, v_ref, kc_ref, vc_ref, bias_ref, y_ref, kb_ref, vb_ref, kcb_ref, vcb_ref,
               *, rows_per_step, img_rows):
    step = pl.program_id(2)
    lane = lax.broadcasted_iota(jnp.int32, (1, LANES), 1)
    m0 = jnp.where(lane < HEAD, 1.0, 0.0)
    kh = min(NA_WIN_H, img_rows)

    @pl.when(step == 0)
    def _():
        kb_ref[...] = k_ref[...].astype(BF16)
        vb_ref[...] = v_ref[...].astype(BF16)
        kcb_ref[...] = kc_ref[...].astype(BF16)
        vcb_ref[...] = vc_ref[...].astype(BF16)

    qs, k_loc, v_loc, bias = [], [], [], []
    for yy in range(rows_per_step):
        y = step * rows_per_step + yy
        y0 = jnp.clip(y - kh // 2, 0, img_rows - kh)
        start = pl.multiple_of(y0 * GRID_W, GRID_W)
        k_loc.append(kb_ref[pl.ds(start, kh * GRID_W), :])
        v_loc.append(vb_ref[pl.ds(start, kh * GRID_W), :])
        bias.append(bias_ref[:, y - y0].reshape(2 * GRID_W, kh * GRID_W))
        qs.append(q_ref[yy * GRID_W:(yy + 1) * GRID_W, :])
    outs = _attn_rows(qs, k_loc, v_loc, bias, kcb_ref[...], vcb_ref[...], m0)
    for yy, o in enumerate(outs):
        rs = slice(yy * GRID_W, (yy + 1) * GRID_W)
        y_ref[rs, :] = (o * _silu(g_ref[rs, :])).astype(y_ref.dtype)


def _ctx_attn_kernel(q_ref, g_ref, kc_ref, vc_ref, y_in_ref, y_ref):
    del y_in_ref
    lane = lax.broadcasted_iota(jnp.int32, (1, LANES), 1)
    m0 = jnp.where(lane < HEAD, 1.0, 0.0)
    o, = _attn_rows([q_ref[...]], None, None, None, kc_ref[...].astype(BF16), vc_ref[...].astype(BF16), m0)
    y_ref[...] = (o * _silu(g_ref[...])).astype(y_ref.dtype)


def _na_bias_table(rpb, img_rows):
    kh = min(NA_WIN_H, img_rows)
    cols = np.arange(GRID_W)
    col_start = np.clip(cols - NA_WIN_W // 2, 0, GRID_W - NA_WIN_W)
    kx = np.arange(GRID_W)
    inwin = (kx[None, :] >= col_start[:, None]) & (kx[None, :] < col_start[:, None] + NA_WIN_W)
    dx = kx[None, :] - cols[:, None] + (NA_WIN_W - 1)
    onehot = (np.arange(2 * NA_WIN_W - 1)[:, None, None] == dx[None]) & inwin[None]
    toep = jnp.einsum('hyd,dxk->hyxk', rpb, jnp.asarray(onehot, rpb.dtype), precision=lax.Precision.HIGHEST)
    toep = jnp.where(inwin[None, None], toep, NEG_BIG)
    tabs = [toep[:, NA_WIN_H - 1 - dl:NA_WIN_H - 1 - dl + kh] for dl in range(kh)]
    tab = jnp.stack(tabs, axis=1).transpose(0, 1, 3, 2, 4)
    return tab.reshape(rpb.shape[0], kh, GRID_W, kh * GRID_W)


def _na(proj_b, bias_tab, geom, db):
    m = proj_b.shape[0]
    npair = db // LANES
    img_rows = geom.seq // GRID_W
    rb = _pick(geom.seq, (8 * GRID_W, 4 * GRID_W, 2 * GRID_W, GRID_W))
    n_lat_steps = geom.seq // rb
    nblk = db // LANES
    ctx_blk = geom.n_lat // geom.ctx
    kh = min(NA_WIN_H, img_rows)
    y = pl.pallas_call(
        functools.partial(_na_kernel, rows_per_step=rb // GRID_W, img_rows=img_rows),
        out_shape=jax.ShapeDtypeStruct((m, db), BF16),
        grid=(geom.batch, npair, n_lat_steps),
        in_specs=[
            pl.BlockSpec((rb, LANES), lambda b, p, s: (b * n_lat_steps + s, 1 * nblk + p)),
            pl.BlockSpec((rb, LANES), lambda b, p, s: (b * n_lat_steps + s, 4 * nblk + p)),
            pl.BlockSpec((geom.seq, LANES), lambda b, p, s: (b, 2 * nblk + p)),
            pl.BlockSpec((geom.seq, LANES), lambda b, p, s: (b, 3 * nblk + p)),
            pl.BlockSpec((geom.ctx, LANES), lambda b, p, s: (ctx_blk + b, 2 * nblk + p)),
            pl.BlockSpec((geom.ctx, LANES), lambda b, p, s: (ctx_blk + b, 3 * nblk + p)),
            pl.BlockSpec((2, kh, GRID_W, kh * GRID_W), lambda b, p, s: (p, 0, 0, 0)),
        ],
        out_specs=pl.BlockSpec((rb, LANES), lambda b, p, s: (b * n_lat_steps + s, p)),
        scratch_shapes=[pltpu.VMEM((geom.seq, LANES), BF16), pltpu.VMEM((geom.seq, LANES), BF16),
                        pltpu.VMEM((geom.ctx, LANES), BF16), pltpu.VMEM((geom.ctx, LANES), BF16)],
        compiler_params=_cparams(("parallel", "parallel", "arbitrary")),
        name="nbr_attention",
    )(proj_b, proj_b, proj_b, proj_b, proj_b, proj_b, bias_tab)
    ctx_spec = lambda sec: pl.BlockSpec((geom.ctx, LANES), lambda b, p: (ctx_blk + b, sec * nblk + p))
    return pl.pallas_call(
        _ctx_attn_kernel,
        out_shape=jax.ShapeDtypeStruct((m, db), BF16),
        grid=(geom.batch, npair),
        in_specs=[ctx_spec(1), ctx_spec(4), ctx_spec(2), ctx_spec(3), pl.BlockSpec(memory_space=pl.ANY)],
        out_specs=pl.BlockSpec((geom.ctx, LANES), lambda b, p: (ctx_blk + b, p)),
        input_output_aliases={4: 0},
        compiler_params=_cparams(("parallel", "parallel")),
        name="ctx_attention",
    )(proj_b, proj_b, proj_b, proj_b, y)


def _rglru_kernel(*refs, geom, tc, gwc, reverse):
    if reverse:
        (x_ref, hp_ref, hn_ref, cw_ref, cb_ref, wa_ref, ba_ref, wx_ref, bx_ref, lam_ref, hf_ref, g_ref,
         out_ref, carry_ref, sa_ref, sb_ref, so_ref) = refs
    else:
        (x_ref, hp_ref, hn_ref, cw_ref, cb_ref, wa_ref, ba_ref, wx_ref, bx_ref, lam_ref,
         out_ref, carry_ref, sa_ref, sb_ref, so_ref) = refs
    b, c = pl.program_id(0), pl.program_id(2)

    @pl.when(c == 0)
    def _():
        carry_ref[...] = jnp.zeros(carry_ref.shape, F32)

    row0 = geom.chunk_block(b, c, tc, reverse) * tc
    is_first, is_last = geom.seq_edges(row0, tc)
    x = x_ref[...]
    nb = _shift_rows(x, hp_ref, hn_ref, slice(None), is_first, is_last, CONV_LEFT, CONV_W - 1 - CONV_LEFT)
    nb[0] = x
    u = cb_ref[...]
    for j in range(CONV_W):
        u = u + cw_ref[j:j + 1, :] * nb[j - CONV_LEFT]
    gate_r = _sigmoid(_bdot(u, wa_ref[0]) + ba_ref[...])
    gate_i = _sigmoid(_bdot(u, wx_ref[0]) + bx_ref[...])
    neg_lam = -lam_ref[...]
    softplus = jnp.maximum(neg_lam, 0.0) + jnp.log1p(jnp.exp(-jnp.abs(neg_lam)))
    log_a = (-RGLRU_C) * gate_r * softplus
    a = jnp.exp(log_a)
    th = jnp.tanh(log_a)
    bv = jnp.sqrt(-2.0 * th / (1.0 - th)) * (gate_i * u)
    for j in range(gwc // LANES):
        sa_ref[j] = a[:, j * LANES:(j + 1) * LANES]
        sb_ref[j] = bv[:, j * LANES:(j + 1) * LANES]
    span = SEG * SUBLANES
    r8 = lax.broadcasted_iota(jnp.int32, (SUBLANES, 1), 0)
    first, last = (SUBLANES - 1, 0) if reverse else (0, SUBLANES - 1)
    steps = list(reversed(range(SEG))) if reverse else list(range(SEG))
    for j in range(gwc // LANES):
        carry = carry_ref[0:1, j * LANES:(j + 1) * LANES]
        for q in (reversed(range(tc // span)) if reverse else range(tc // span)):
            rows = [pl.ds(q * span + i, SUBLANES, stride=SEG) for i in range(SEG)]
            av = [sa_ref[j, rw, :] for rw in rows]
            bw = [sb_ref[j, rw, :] for rw in rows]
            hs, ps = {}, {}
            h_run, p_run = bw[steps[0]], av[steps[0]]
            hs[steps[0]], ps[steps[0]] = h_run, p_run
            for i in steps[1:]:
                h_run = av[i] * h_run + bw[i]
                p_run = av[i] * p_run
                hs[i], ps[i] = h_run, p_run
            pp, hh = p_run, h_run
            s_ = 1
            while s_ < SUBLANES:
                if reverse:
                    p_s, h_s = pltpu.roll(pp, SUBLANES - s_, 0), pltpu.roll(hh, SUBLANES - s_, 0)
                    valid = r8 < SUBLANES - s_
                else:
                    p_s, h_s = pltpu.roll(pp, s_, 0), pltpu.roll(hh, s_, 0)
                    valid = r8 >= s_
                hh = jnp.where(valid, pp * h_s + hh, hh)
                pp = jnp.where(valid, pp * p_s, pp)
                s_ *= 2
            state = hh + pp * carry
            shifted = pltpu.roll(state, (SUBLANES - 1) if reverse else 1, 0)
            cin = jnp.where(r8 == first, carry, shifted)
            carry = state[last:last + 1, :]
            for i in range(SEG):
                so_ref[j, rows[i], :] = hs[i] + ps[i] * cin
        carry_ref[:, j * LANES:(j + 1) * LANES] = jnp.broadcast_to(carry, (SUBLANES, LANES))
    h = jnp.concatenate([so_ref[j] for j in range(gwc // LANES)], axis=1)
    if reverse:
        out_ref[...] = ((hf_ref[...] + h) * _silu(g_ref[...])).astype(out_ref.dtype)
    else:
        out_ref[...] = h


def _rglru(proj_c, geom, tc, gwc, conv_w, conv_b, wa_g, ba, wx_g, bx, lam, reverse, h_f=None):
    m = proj_c.shape[0]
    dc = conv_w.shape[1]
    ngrp = dc // gwc
    nch = (geom.seq + geom.ctx) // tc
    nb8 = m // SUBLANES
    r8 = tc // SUBLANES
    blk = lambda b, g, c: geom.chunk_block(b, c, tc, reverse)
    vec = lambda rows: pl.BlockSpec((rows, gwc), lambda b, g, c: (0, g))
    in_specs = [
        pl.BlockSpec((tc, gwc), lambda b, g, c: (blk(b, g, c), g)),
        pl.BlockSpec((SUBLANES, gwc), lambda b, g, c: (jnp.maximum(blk(b, g, c) * r8 - 1, 0), g)),
        pl.BlockSpec((SUBLANES, gwc), lambda b, g, c: (jnp.minimum((blk(b, g, c) + 1) * r8, nb8 - 1), g)),
        vec(CONV_W), vec(1),
        pl.BlockSpec((1, gwc, gwc), lambda b, g, c: (g, 0, 0)), vec(1),
        pl.BlockSpec((1, gwc, gwc), lambda b, g, c: (g, 0, 0)), vec(1),
        vec(1),
    ]
    args = [proj_c, proj_c, proj_c, conv_w, conv_b, wa_g, ba, wx_g, bx, lam]
    if reverse:
        in_specs += [
            pl.BlockSpec((tc, gwc), lambda b, g, c: (blk(b, g, c), g)),
            pl.BlockSpec((tc, gwc), lambda b, g, c: (blk(b, g, c), ngrp + g)),
        ]
        args += [h_f, proj_c]
    return pl.pallas_call(
        functools.partial(_rglru_kernel, geom=geom, tc=tc, gwc=gwc, reverse=reverse),
        out_shape=jax.ShapeDtypeStruct((m, dc), BF16 if reverse else F32),
        grid=(geom.batch, ngrp, nch),
        in_specs=in_specs,
        out_specs=pl.BlockSpec((tc, gwc), lambda b, g, c: (blk(b, g, c), g)),
        scratch_shapes=[pltpu.VMEM((SUBLANES, gwc), F32)] + [pltpu.VMEM((gwc // LANES, tc, LANES), F32)] * 3,
        compiler_params=_cparams(("parallel", "parallel", "arbitrary")),
        name="rglru_bwd" if reverse else "rglru_fwd",
    )(*args)


def _group_blockdiag(w, gwc):
    nblk, bs, _ = w.shape
    per = gwc // bs
    wg = w.reshape(nblk // per, per, bs, bs)
    eye = jnp.eye(per, dtype=w.dtype)
    dense = jnp.einsum('gpij,pq->gpiqj', wg, eye)
    return dense.reshape(nblk // per, gwc, gwc)


def kernel(x, c, ctx, c_ctx, mod_w, mod_b, norm_pre, norm_post, ev_w_in, ev_mu, ev_w0, ev_w_up, ev_a0, ev_a_up, ev_k_k, ev_k_a, ev_r_k, ev_gn_w, ev_gn_b, ev_rpb, ev_w_out, od_w_in, od_conv_w, od_conv_b, od_gate_a_w, od_gate_a_b, od_gate_x_w, od_gate_x_b, od_lambda, od_w_out):
    batch, seq, d = x.shape
    n_ctx = ctx.shape[1]
    depth = mod_w.shape[0]
    da = ev_k_k.shape[1]
    rw = ev_w_up.shape[2]
    db = ev_w_out.shape[1] - da
    dc = od_conv_w.shape[2]
    rg_bs = od_gate_a_w.shape[-1]
    assert da % LANES == 0 and db == da and 2 * rw == LANES and ev_a_up.shape[2] == rw
    assert batch < SUBLANES and seq % GRID_W == 0 and seq % n_ctx == 0 and n_ctx % CHUNK == 0
    assert n_ctx % GRID_W == 0 and (batch * seq) % n_ctx == 0
    geom = _Geom(batch, seq, n_ctx)
    m, n_lat = geom.m, geom.n_lat

    tm = _pick(math.gcd(n_lat, batch * n_ctx), (512, 256, 128, 64))
    tp = _pick(math.gcd(seq, n_ctx), (256, 128, 64))
    lat_blocks = n_lat // tm
    per_batch = seq // tm
    mod_row = lambda i: jnp.where(i < lat_blocks, i // per_batch, batch)

    c8 = jnp.zeros((SUBLANES, d), F32).at[:batch].set(c).at[batch].set(c_ctx)
    mods = _modulation(c8, mod_w, mod_b).reshape(depth, SUBLANES, 1, 3 * d)

    x_all = jnp.concatenate([x.reshape(n_lat, d), ctx.reshape(batch * n_ctx, d)], axis=0)

    heads_a = da // HEAD
    sel_np = np.zeros((da, LANES), np.float32)
    sel_np[np.arange(da), np.arange(da) // HEAD] = 1.0
    sel = jnp.asarray(sel_np, BF16)
    selt = jnp.asarray(sel_np.T, BF16)
    assert heads_a <= LANES

    a_shifted = 3 * da + 4 * rw
    gwc = rg_bs * LANES // math.gcd(rg_bs, LANES)

    for layer in range(depth):
        last = layer == depth - 1
        i = layer // 2
        mods_l = mods[layer]
        g_pre = norm_pre[layer].reshape(1, d)
        g_post = norm_post[layer].reshape(1, d)
        m_out = n_lat if last else m
        h = _normmod(x_all, mods_l, g_pre, tm, mod_row)
        if layer % 2 == 0:
            w_in = ev_w_in[i].astype(BF16)
            proj_a = _inproj(h, w_in[:, :a_shifted])
            proj_b = _inproj(h, w_in[:, a_shifted:])
            pad_dir = lambda w: jnp.stack([
                jnp.concatenate([w[0], jnp.zeros_like(w[1])], axis=0),
                jnp.concatenate([jnp.zeros_like(w[0]), w[1]], axis=0)]).astype(BF16)
            r, v, kap, kd, bb, lw = _rwkv_prep(
                proj_a, geom, tp, ev_mu[i], ev_w0[i], pad_dir(ev_w_up[i]), ev_a0[i], pad_dir(ev_a_up[i]),
                ev_k_k[i].reshape(1, da), ev_k_a[i].reshape(1, da), sel, selt)
            o_f, o_b = _rwkv_scan(r, v, kap, kd, bb, lw, geom)
            y_a = _rwkv_readout(o_f, o_b, r, v, kd, proj_b, ev_r_k[i].reshape(1, da),
                                ev_gn_w[i].reshape(1, da), ev_gn_b[i].reshape(1, da), sel, selt, tp)
            y_b = _na(proj_b, _na_bias_table(ev_rpb[i], seq // GRID_W), geom, db)
            w_out = ev_w_out[i].astype(BF16)
            x_all = _outproj([y_a, y_b], [w_out[:da], w_out[da:]], x_all, mods_l, g_post, tm, mod_row, m_out)
        else:
            proj_c = _inproj(h, od_w_in[i].astype(BF16))
            row = lambda a: a.reshape(1, dc)
            h_f = None
            for dd in range(2):
                out = _rglru(proj_c, geom, tp, gwc, od_conv_w[i], row(od_conv_b[i]),
                             _group_blockdiag(od_gate_a_w[i, dd], gwc).astype(BF16), row(od_gate_a_b[i, dd]),
                             _group_blockdiag(od_gate_x_w[i, dd], gwc).astype(BF16), row(od_gate_x_b[i, dd]),
                             row(od_lambda[i, dd]), dd == 1, h_f)
                h_f = out
            x_all = _outproj([out], [od_w_out[i].astype(BF16)], x_all, mods_l, g_post, tm, mod_row, m_out)
    return x_all.reshape(batch, seq, d)
```

```python
import functools
import math

import jax
import jax.numpy as jnp
import numpy as np
from jax import lax
from jax.experimental import pallas as pl
from jax.experimental.pallas import tpu as pltpu

F32 = jnp.float32
BF16 = jnp.bfloat16

EPS_RMS = 1e-6
GN_EPS = 64e-5
RGLRU_C = 8.0
HEAD = 64
GRID_W = 64
NA_WIN_H = 8
NA_WIN_W = 16
CONV_W = 4
CONV_LEFT = 2
LANES = 128
SUBLANES = 8
SEG = 4
CHUNK = 64
NEG_BIG = -1e30
VMEM_LIMIT = 56 * 1024 * 1024


def _cparams(sem):
    return pltpu.CompilerParams(dimension_semantics=sem, vmem_limit_bytes=VMEM_LIMIT)


def _pick(n, cands):
    for c in cands:
        if n % c == 0:
            return c
    raise ValueError(f"no tile for {n} in {cands}")


def _bdot(a, b):
    return jnp.dot(a.astype(BF16), b.astype(BF16), preferred_element_type=F32)


def _bdot_nt(a, b):
    return lax.dot_general(a.astype(BF16), b.astype(BF16), (((1,), (1,)), ((), ())),
                           preferred_element_type=F32)


def _bdot_tn(a, b):
    return lax.dot_general(a.astype(BF16), b.astype(BF16), (((0,), (0,)), ((), ())),
                           preferred_element_type=F32)


def _split3(x):
    hi = x.astype(BF16)
    r1 = x - hi.astype(F32)
    mid = r1.astype(BF16)
    lo = (r1 - mid.astype(F32)).astype(BF16)
    return hi, mid, lo


def _dot_sel_r(x, sel):
    hi, mid, lo = _split3(x)
    d = lambda t: jnp.dot(t, sel, preferred_element_type=F32)
    return d(hi) + d(mid) + d(lo)


def _cumsum_rows(x, reverse):
    n = x.shape[0] // SUBLANES
    x3 = x.reshape(n, SUBLANES, x.shape[1])
    r8 = lax.broadcasted_iota(jnp.int32, (1, SUBLANES, 1), 1)
    s = 1
    while s < SUBLANES:
        if reverse:
            x3 = x3 + jnp.where(r8 < SUBLANES - s, pltpu.roll(x3, SUBLANES - s, 1), 0.0)
        else:
            x3 = x3 + jnp.where(r8 >= s, pltpu.roll(x3, s, 1), 0.0)
        s *= 2
    edge = 0 if reverse else SUBLANES - 1
    groups, run = [None] * n, None
    for gi in (reversed(range(n)) if reverse else range(n)):
        groups[gi] = x3[gi] if run is None else x3[gi] + run
        run = groups[gi][edge:edge + 1, :]
    return jnp.concatenate(groups, axis=0)


def _sigmoid(x):
    return 1.0 / (1.0 + jnp.exp(-x))


def _silu(x):
    return x * _sigmoid(x)


def _mod_kernel(c_ref, w_ref, b_ref, o_ref):
    cs = _silu(c_ref[...])
    o_ref[...] = _bdot(cs, w_ref[...]) + b_ref[...]


def _modulation(c8, mod_w, mod_b):
    depth, d, n = mod_w.shape
    tn = _pick(n, (1024, 512, 256, 128))
    return pl.pallas_call(
        _mod_kernel,
        out_shape=jax.ShapeDtypeStruct((depth, SUBLANES, n), F32),
        grid=(depth, n // tn),
        in_specs=[
            pl.BlockSpec((SUBLANES, d), lambda l, j: (0, 0)),
            pl.BlockSpec((None, d, tn), lambda l, j: (l, 0, j)),
            pl.BlockSpec((None, 1, tn), lambda l, j: (l, 0, j)),
        ],
        out_specs=pl.BlockSpec((None, SUBLANES, tn), lambda l, j: (l, 0, j)),
        compiler_params=_cparams(("parallel", "parallel")),
        name="modulation",
    )(c8, mod_w, mod_b.reshape(depth, 1, n))


def _stream_specs(xs, tm):
    d = xs[0].shape[1]
    if len(xs) == 1:
        return [pl.BlockSpec((tm, d), lambda i: (i, 0))]
    nl = xs[0].shape[0] // tm
    return [pl.BlockSpec((tm, d), lambda i: (jnp.minimum(i, nl - 1), 0)),
            pl.BlockSpec((tm, d), lambda i: (jnp.maximum(i - nl, 0), 0))]


def _stream_block(x_refs, n_lat_blocks):
    if len(x_refs) == 1:
        return x_refs[0][...]
    return jnp.where(pl.program_id(0) < n_lat_blocks, x_refs[0][...], x_refs[1][...])


def _normmod_kernel(*refs, d, n_lat_blocks):
    x_refs, (mod_ref, g_ref, h_ref) = refs[:-3], refs[-3:]
    x = _stream_block(x_refs, n_lat_blocks)
    ms = jnp.mean(x * x, axis=-1, keepdims=True)
    y = x * lax.rsqrt(ms + EPS_RMS) * g_ref[...]
    h_ref[...] = (y * (1.0 + mod_ref[:, d:2 * d]) + mod_ref[:, 0:d]).astype(h_ref.dtype)


def _normmod(xs, mods_l, g, tm, mod_row):
    m, d = sum(x.shape[0] for x in xs), xs[0].shape[1]
    return pl.pallas_call(
        functools.partial(_normmod_kernel, d=d, n_lat_blocks=xs[0].shape[0] // tm),
        out_shape=jax.ShapeDtypeStruct((m, d), BF16),
        grid=(m // tm,),
        in_specs=_stream_specs(xs, tm) + [
            pl.BlockSpec((None, 1, 3 * d), lambda i: (mod_row(i), 0, 0)),
            pl.BlockSpec((1, d), lambda i: (0, 0)),
        ],
        out_specs=pl.BlockSpec((tm, d), lambda i: (i, 0)),
        compiler_params=_cparams(("parallel",)),
        name="normmod",
    )(*xs, mods_l, g)


def _inproj_kernel(h_ref, w_ref, o_ref):
    o_ref[...] = jnp.dot(h_ref[...], w_ref[...], preferred_element_type=F32)


def _inproj(h, w_bf):
    m, d = h.shape
    n = w_bf.shape[1]
    tm = _pick(m, (1024, 512, 256, 128, 64))
    tn = _pick(n, (1664, 1408, 1280, 1024, 768, 640, 512, 384, 256, 128))
    return pl.pallas_call(
        _inproj_kernel,
        out_shape=jax.ShapeDtypeStruct((m, n), F32),
        grid=(n // tn, m // tm),
        in_specs=[
            pl.BlockSpec((tm, d), lambda j, i: (i, 0)),
            pl.BlockSpec((d, tn), lambda j, i: (0, j)),
        ],
        out_specs=pl.BlockSpec((tm, tn), lambda j, i: (i, j)),
        compiler_params=_cparams(("parallel", "parallel")),
        name="inproj",
    )(h, w_bf)


def _outproj_kernel(*refs, n_in, d, n_lat_blocks):
    y_refs = refs[:n_in]
    w_refs = refs[n_in:2 * n_in]
    x_refs, (mod_ref, g_ref, o_ref) = refs[2 * n_in:-3], refs[-3:]
    acc = jnp.dot(y_refs[0][...], w_refs[0][...], preferred_element_type=F32)
    for i in range(1, n_in):
        acc = acc + jnp.dot(y_refs[i][...], w_refs[i][...], preferred_element_type=F32)
    ms = jnp.mean(acc * acc, axis=-1, keepdims=True)
    z = acc * lax.rsqrt(ms + EPS_RMS) * g_ref[...]
    o_ref[...] = _stream_block(x_refs, n_lat_blocks) + mod_ref[:, 2 * d:3 * d] * z


def _outproj(ys, ws, xs, mods_l, g, tm, mod_row, m_out):
    d = xs[0].shape[1]
    n_in = len(ys)
    in_specs = [pl.BlockSpec((tm, y.shape[1]), lambda i: (i, 0)) for y in ys]
    in_specs += [pl.BlockSpec(w.shape, lambda i: (0, 0)) for w in ws]
    in_specs += _stream_specs(xs, tm) + [
        pl.BlockSpec((None, 1, 3 * d), lambda i: (mod_row(i), 0, 0)),
        pl.BlockSpec((1, d), lambda i: (0, 0)),
    ]
    return pl.pallas_call(
        functools.partial(_outproj_kernel, n_in=n_in, d=d, n_lat_blocks=xs[0].shape[0] // tm),
        out_shape=jax.ShapeDtypeStruct((m_out, d), F32),
        grid=(m_out // tm,),
        in_specs=in_specs,
        out_specs=pl.BlockSpec((tm, d), lambda i: (i, 0)),
        compiler_params=_cparams(("parallel",)),
        name="outproj",
    )(*ys, *ws, *xs, mods_l, g)


class _Geom:
    def __init__(self, batch, seq, ctx):
        self.batch, self.seq, self.ctx = batch, seq, ctx
        self.n_lat = batch * seq
        self.m = batch * (seq + ctx)

    def seq_edges(self, row0, rows):
        in_lat = row0 < self.n_lat
        off = jnp.where(in_lat, row0 % self.seq, (row0 - self.n_lat) % self.ctx)
        length = jnp.where(in_lat, self.seq, self.ctx)
        return off == 0, off + rows == length

    def chunk_block(self, b, c, rows, reverse):
        cc = self.ctx // rows
        sc = self.seq // rows
        if reverse:
            ctx_blk = self.n_lat // rows + b * cc + (cc - 1 - c)
            lat_blk = b * sc + (sc - 1 - (c - cc))
        else:
            ctx_blk = self.n_lat // rows + b * cc + c
            lat_blk = b * sc + (c - cc)
        return jnp.where(c < cc, ctx_blk, lat_blk)


def _shift_rows(x, hp_ref, hn_ref, cols, is_first, is_last, back, fwd):
    rows = x.shape[0]
    r8 = lax.broadcasted_iota(jnp.int32, (SUBLANES, 1), 0)
    hp = jnp.where(is_first, 0.0, hp_ref[:, cols])
    hn = jnp.where(is_last, 0.0, hn_ref[:, cols])
    outs = {}
    for s in range(1, back + 1):
        y = pltpu.roll(x, s, 0)
        head = jnp.where(r8 < s, pltpu.roll(hp, s, 0), y[:SUBLANES])
        outs[-s] = jnp.concatenate([head, y[SUBLANES:]], axis=0)
    for s in range(1, fwd + 1):
        y = pltpu.roll(x, rows - s, 0)
        tail = jnp.where(r8 >= SUBLANES - s, pltpu.roll(hn, SUBLANES - s, 0), y[rows - SUBLANES:])
        outs[s] = jnp.concatenate([y[:rows - SUBLANES], tail], axis=0)
    return outs


def _halo_specs(tp, width, m):
    nb8 = m // SUBLANES
    r8 = tp // SUBLANES
    return [
        pl.BlockSpec((SUBLANES, width), lambda i: (jnp.maximum(i * r8 - 1, 0), 0)),
        pl.BlockSpec((SUBLANES, width), lambda i: (jnp.minimum((i + 1) * r8, nb8 - 1), 0)),
    ]


def _rwkv_prep_kernel(f_ref, hp_ref, hn_ref, mu_ref, w0_ref, wup_ref, a0_ref, aup_ref, kk_ref, ka_ref,
                      sel_ref, selt_ref, r_ref, v_ref, kap_ref, kd_ref, bb_ref, lw_ref, *, geom, tp, da):
    i = pl.program_id(0)
    is_first, is_last = geom.seq_edges(i * tp, tp)

    def shifted(lo, hi):
        cols = slice(lo, hi)
        x = f_ref[:, cols]
        nb = _shift_rows(x, hp_ref, hn_ref, cols, is_first, is_last, 1, 1)
        return x + mu_ref[0:1, cols] * (nb[-1] - x) + mu_ref[1:2, cols] * (nb[1] - x)

    r = shifted(0, da)
    k = shifted(da, 2 * da)
    v = shifted(2 * da, 3 * da)
    cw = jnp.tanh(shifted(3 * da, 3 * da + LANES))
    ca = shifted(3 * da + LANES, 3 * da + 2 * LANES)
    r_ref[...] = r.astype(r_ref.dtype)
    v_ref[...] = v.astype(v_ref.dtype)
    kk = k * kk_ref[...]
    ss = _dot_sel_r(kk * kk, sel_ref[...])
    inv = lax.rsqrt(ss + 1e-12)
    kap = kk * _dot_sel_r(inv, selt_ref[...])
    kap_ref[...] = kap.astype(kap_ref.dtype)
    for dd in range(2):
        zw = w0_ref[dd:dd + 1, :] + _bdot(cw, wup_ref[dd])
        lw_ref[dd] = -math.exp(-0.5) * _sigmoid(zw)
        a = _sigmoid(a0_ref[dd:dd + 1, :] + _bdot(ca, aup_ref[dd]))
        kd_ref[dd] = (k * (1.0 + (a - 1.0) * ka_ref[...])).astype(kd_ref.dtype)
        bb_ref[dd] = (kap * a).astype(bb_ref.dtype)


def _rwkv_prep(proj_a, geom, tp, mu, w0, wup_pad, a0, aup_pad, k_k, k_a, sel, selt):
    m, wa = proj_a.shape
    da = k_k.shape[1]
    full = lambda a: pl.BlockSpec(a.shape, lambda i: (0,) * a.ndim)
    tok = jax.ShapeDtypeStruct((m, da), BF16)
    tok2 = jax.ShapeDtypeStruct((2, m, da), BF16)
    params = (mu, w0, wup_pad, a0, aup_pad, k_k, k_a, sel, selt)
    return pl.pallas_call(
        functools.partial(_rwkv_prep_kernel, geom=geom, tp=tp, da=da),
        out_shape=(tok, tok, tok, tok2, tok2, jax.ShapeDtypeStruct((2, m, da), F32)),
        grid=(m // tp,),
        in_specs=[pl.BlockSpec((tp, wa), lambda i: (i, 0))] + _halo_specs(tp, wa, m) + [full(p) for p in params],
        out_specs=(
            pl.BlockSpec((tp, da), lambda i: (i, 0)),
            pl.BlockSpec((tp, da), lambda i: (i, 0)),
            pl.BlockSpec((tp, da), lambda i: (i, 0)),
            pl.BlockSpec((2, tp, da), lambda i: (0, i, 0)),
            pl.BlockSpec((2, tp, da), lambda i: (0, i, 0)),
            pl.BlockSpec((2, tp, da), lambda i: (0, i, 0)),
        ),
        compiler_params=_cparams(("parallel",)),
        name="rwkv_prep",
    )(proj_a, proj_a, proj_a, *params)


def _rwkv_chunk_terms(jobs):
    L = CHUNK
    each = lambda f, *cols: [f(*a) for a in zip(*cols)]
    r, v, kap, kd, bb, lw, consts = (list(c) for c in zip(*jobs))
    reverse, strict, incl, bd_full, m0, m1, eye, last_row, levels = (list(c) for c in zip(*consts))
    stack = lambda x: jnp.concatenate([x * m0[0], x * m1[0]], axis=0).astype(BF16)

    cum = each(_cumsum_rows, lw, reverse)
    tot = each(lambda c, lr: c[lr:lr + 1, :], cum, last_row)
    rt = each(lambda x, c: (x * jnp.exp(c)).astype(BF16), r, cum)
    e_neg = each(lambda c: jnp.exp(-c), cum)
    kt = each(lambda x, e: x * e, kd, e_neg)
    bt = each(lambda x, e: x * e, bb, e_neg)
    kapt = each(lambda x, c, l: (x * jnp.exp(c - l)).astype(BF16), kap, cum, lw)
    vv_s = each(stack, v)
    g = each(lambda ka, x, b_, k_: _bdot_nt(jnp.concatenate([ka, x], axis=0),
                                            jnp.concatenate([stack(b_), stack(k_)], axis=0)),
             kapt, rt, bt, kt)
    n = each(lambda m_, x: jnp.where(m_, -x[:L, :2 * L], 0.0), strict, g)
    mk = each(lambda m_, x: jnp.where(m_, x[:L, 2 * L:], 0.0), strict, g)
    arb = each(lambda m_, x: jnp.where(m_, x[L:, :2 * L], 0.0), incl, g)
    ark = each(lambda m_, x: jnp.where(m_, x[L:, 2 * L:], 0.0), incl, g)
    mkz = each(lambda m_, a_, vv: _bdot(jnp.concatenate([m_, a_], axis=0), vv), mk, ark, vv_s)
    tm = each(lambda e, lv, x: e + jnp.where(lv[0], x, 0.0), eye, levels, n)
    for li in range(1, len(levels[0])):
        x = each(lambda lv, n_, t: _bdot(jnp.where(lv[li], n_, 0.0), stack(t)), levels, n, tm)
        tm = each(lambda t, x_: t + _bdot(t, stack(x_)), tm, x)
    wy = each(lambda t, ka, x: _bdot(t, jnp.concatenate([stack(ka), stack(x[:L])], axis=1)), tm, kapt, mkz)
    z = each(lambda x: x[L:], mkz)
    rest = each(lambda t, c: jnp.exp(t - c), tot, cum)
    gk = each(lambda v_, k_, e: _bdot_tn(v_, k_ * e), v, kd, rest)
    wrt = each(lambda x, rt_: jnp.concatenate([x[:, :LANES].astype(BF16), rt_], axis=0), wy, rt)
    y = each(lambda x: x[:, LANES:], wy)
    bp = each(lambda b_, e: (b_ * e).astype(BF16), bb, rest)
    return list(zip(wrt, y, arb, z, bp, gk, tot))


def _rwkv_chunk_states(s_prev, terms, consts):
    L = CHUNK
    each = lambda f, *cols: [f(*a) for a in zip(*cols)]
    wrt, y, arb, z, bp, gk, tot = (list(c) for c in zip(*terms))
    bd_full, m0, m1 = consts[3], consts[4], consts[5]
    stack = lambda x: jnp.concatenate([x * m0, x * m1], axis=0).astype(BF16)
    wr = each(_bdot_nt, wrt, s_prev)
    u = each(lambda x, y_: -(x[:L] + y_), wr, y)
    o = each(lambda x, a, u_, z_: x[L:] + _bdot(a, stack(u_)) + z_, wr, arb, u, z)
    s_new = each(lambda s, t, u_, b_, g_: jnp.where(bd_full, s * jnp.exp(t) + _bdot_tn(u_, b_) + g_, 0.0),
                 s_prev, tot, u, bp, gk)
    return s_new, o


def _rwkv_consts(reverse):
    L = CHUNK
    ri = lax.broadcasted_iota(jnp.int32, (L, 2 * L), 0)
    ci = lax.broadcasted_iota(jnp.int32, (L, 2 * L), 1) & (L - 1)
    strict = (ci > ri) if reverse else (ci < ri)
    incl = (ci >= ri) if reverse else (ci <= ri)
    lane = lax.broadcasted_iota(jnp.int32, (1, LANES), 1)
    m0 = jnp.where(lane < HEAD, 1.0, 0.0)
    m1 = 1.0 - m0
    eye = jnp.where(ri == ci, 1.0, 0.0)
    rs = lax.broadcasted_iota(jnp.int32, (2 * HEAD, 2 * HEAD), 0)
    cs = lax.broadcasted_iota(jnp.int32, (2 * HEAD, 2 * HEAD), 1)
    bd_full = (rs >= HEAD) == (cs >= HEAD)
    levels = tuple(((ri >> (l + 1)) == (ci >> (l + 1))) & ((ri >> l) != (ci >> l))
                   for l in range(int(math.log2(L))))
    return (reverse, strict, incl, bd_full, m0, m1, eye, 0 if reverse else L - 1, levels)


def _rwkv_scan_kernel(*refs, npairs, nsub):
    ins = refs[:12]
    of_ref, ob_ref, s_ref = refs[12:]

    @pl.when(pl.program_id(2) == 0)
    def _():
        s_ref[...] = jnp.zeros(s_ref.shape, F32)

    consts = [_rwkv_consts(False), _rwkv_consts(True)]
    chains = [(dd, p) for dd in range(2) for p in range(npairs)]
    sub = lambda dd, k: (nsub - 1 - k) if dd == 1 else k
    jobs = []
    for k in range(nsub):
        for dd, p in chains:
            r_ref, v_ref, kap_ref, kd_ref, bb_ref, lw_ref = ins[6 * dd:6 * dd + 6]
            rs = slice(sub(dd, k) * CHUNK, (sub(dd, k) + 1) * CHUNK)
            sl = slice(p * LANES, (p + 1) * LANES)
            jobs.append((r_ref[rs, sl], v_ref[rs, sl], kap_ref[rs, sl], kd_ref[rs, sl], bb_ref[rs, sl],
                         lw_ref[rs, sl], consts[dd]))
    terms = _rwkv_chunk_terms(jobs)
    states = [s_ref[dd, p] for dd, p in chains]
    for k in range(nsub):
        states, outs = _rwkv_chunk_states(states, terms[k * len(chains):(k + 1) * len(chains)], consts[0])
        for (dd, p), o in zip(chains, outs):
            o_ref = ob_ref if dd == 1 else of_ref
            o_ref[sub(dd, k) * CHUNK:(sub(dd, k) + 1) * CHUNK, p * LANES:(p + 1) * LANES] = o
    for (dd, p), s_new in zip(chains, states):
        s_ref[dd, p] = s_new


def _rwkv_scan(r, v, kap, kd, bb, lw, geom):
    m, da = r.shape
    gw = _pick(da, (1024, 512, 256, 128))
    npairs = gw // LANES
    rows = _pick(geom.ctx, (4 * CHUNK, 2 * CHUNK, CHUNK))
    nsteps = (geom.seq + geom.ctx) // rows
    in_specs = []
    for dd in range(2):
        blk = lambda b, g, c, dd=dd: geom.chunk_block(b, c, rows, dd == 1)
        tok = pl.BlockSpec((rows, gw), lambda b, g, c, blk=blk: (blk(b, g, c), g))
        tokd = pl.BlockSpec((None, rows, gw), lambda b, g, c, blk=blk, dd=dd: (dd, blk(b, g, c), g))
        in_specs += [tok, tok, tok, tokd, tokd, tokd]
    out_specs = tuple(
        pl.BlockSpec((rows, gw), lambda b, g, c, dd=dd: (geom.chunk_block(b, c, rows, dd == 1), g))
        for dd in range(2))
    tok_shape = jax.ShapeDtypeStruct((m, da), F32)
    return pl.pallas_call(
        functools.partial(_rwkv_scan_kernel, npairs=npairs, nsub=rows // CHUNK),
        out_shape=(tok_shape, tok_shape),
        grid=(geom.batch, da // gw, nsteps),
        in_specs=in_specs,
        out_specs=out_specs,
        scratch_shapes=[pltpu.VMEM((2, npairs, LANES, LANES), F32)],
        compiler_params=_cparams(("parallel", "parallel", "arbitrary")),
        name="rwkv_scan",
    )(r, v, kap, kd, bb, lw, r, v, kap, kd, bb, lw)


def _rwkv_readout_kernel(of_ref, ob_ref, r_ref, v_ref, kd_ref, g_ref, rk_ref, gw_ref, gb_ref,
                         sel_ref, selt_ref, y_ref):
    sel, selt = sel_ref[...], selt_ref[...]
    head_mean = lambda x: _dot_sel_r(_dot_sel_r(x, sel) * (1.0 / HEAD), selt)
    o = of_ref[...] + ob_ref[...]
    cen = o - head_mean(o)
    var = head_mean(cen * cen)
    on = cen * lax.rsqrt(var + GN_EPS) * gw_ref[...] + gb_ref[...]
    bonus = _dot_sel_r(_dot_sel_r(r_ref[...] * (kd_ref[0] + kd_ref[1]) * rk_ref[...], sel), selt)
    y_ref[...] = ((on + bonus * v_ref[...]) * _silu(g_ref[...])).astype(y_ref.dtype)


def _rwkv_readout(o_f, o_b, r, v, kd, proj_b, r_k, gn_w, gn_b, sel, selt, tr):
    m, da = r.shape
    full = lambda a: pl.BlockSpec(a.shape, lambda i: (0,) * a.ndim)
    tok = pl.BlockSpec((tr, da), lambda i: (i, 0))
    params = (r_k, gn_w, gn_b, sel, selt)
    return pl.pallas_call(
        _rwkv_readout_kernel,
        out_shape=jax.ShapeDtypeStruct((m, da), BF16),
        grid=(m // tr,),
        in_specs=[tok, tok, tok, tok, pl.BlockSpec((2, tr, da), lambda i: (0, i, 0)), tok] + [full(p) for p in params],
        out_specs=tok,
        compiler_params=_cparams(("parallel",)),
        name="rwkv_readout",
    )(o_f, o_b, r, v, kd, proj_b, *params)


def _attn_rows(qs, k_loc, v_loc, bias, kc, vc, m0):
    each = lambda f, *cols: [f(*a) for a in zip(*cols)]
    n = qs[0].shape[0]
    q2 = each(lambda q: (jnp.concatenate([q * m0, q * (1.0 - m0)], axis=0) * (HEAD ** -0.5)).astype(BF16), qs)
    rows2 = 2 * n
    split = lambda x: [x[i * rows2:(i + 1) * rows2] for i in range(len(qs))]
    s_ctx = split(_bdot_nt(jnp.concatenate(q2, axis=0), kc))
    mx = each(lambda s: jnp.max(s, axis=-1, keepdims=True), s_ctx)
    if k_loc is not None:
        s_loc = each(lambda q, k, b: _bdot_nt(q, k) + b, q2, k_loc, bias)
        mx = each(lambda m_, s: jnp.maximum(m_, jnp.max(s, axis=-1, keepdims=True)), mx, s_loc)
        p_loc = each(lambda s, m_: jnp.exp(s - m_), s_loc, mx)
    p_ctx = each(lambda s, m_: jnp.exp(s - m_), s_ctx, mx)
    den = each(lambda p: jnp.sum(p, axis=-1, keepdims=True), p_ctx)
    acc = split(_bdot(jnp.concatenate([p.astype(BF16) for p in p_ctx], axis=0), vc))
    if k_loc is not None:
        den = each(lambda d_, p: d_ + jnp.sum(p, axis=-1, keepdims=True), den, p_loc)
        acc = each(lambda a, p, v: a + _bdot(p, v), acc, p_loc, v_loc)
    acc = each(lambda a, d_: a / d_, acc, den)
    return each(lambda a: jnp.where(m0 > 0.5, a[:n], a[n:]), acc)


def _na_kernel(q_ref, g_ref, k_ref, v_ref, kc_ref, vc_ref, bias_ref, y_ref, kb_ref, vb_ref, kcb_ref, vcb_ref,
               *, rows_per_step, img_rows):
    step = pl.program_id(2)
    lane = lax.broadcasted_iota(jnp.int32, (1, LANES), 1)
    m0 = jnp.where(lane < HEAD, 1.0, 0.0)
    kh = min(NA_WIN_H, img_rows)

    @pl.when(step == 0)
    def _():
        kb_ref[...] = k_ref[...].astype(BF16)
        vb_ref[...] = v_ref[...].astype(BF16)
        kcb_ref[...] = kc_ref[...].astype(BF16)
        vcb_ref[...] = vc_ref[...].astype(BF16)

    qs, k_loc, v_loc, bias = [], [], [], []
    for yy in range(rows_per_step):
        y = step * rows_per_step + yy
        y0 = jnp.clip(y - kh // 2, 0, img_rows - kh)
        start = pl.multiple_of(y0 * GRID_W, GRID_W)
        k_loc.append(kb_ref[pl.ds(start, kh * GRID_W), :])
        v_loc.append(vb_ref[pl.ds(start, kh * GRID_W), :])
        bias.append(bias_ref[:, y - y0].reshape(2 * GRID_W, kh * GRID_W))
        qs.append(q_ref[yy * GRID_W:(yy + 1) * GRID_W, :])
    outs = _attn_rows(qs, k_loc, v_loc, bias, kcb_ref[...], vcb_ref[...], m0)
    for yy, o in enumerate(outs):
        rs = slice(yy * GRID_W, (yy + 1) * GRID_W)
        y_ref[rs, :] = (o * _silu(g_ref[rs, :])).astype(y_ref.dtype)


def _ctx_attn_kernel(q_ref, g_ref, kc_ref, vc_ref, y_in_ref, y_ref):
    del y_in_ref
    lane = lax.broadcasted_iota(jnp.int32, (1, LANES), 1)
    m0 = jnp.where(lane < HEAD, 1.0, 0.0)
    o, = _attn_rows([q_ref[...]], None, None, None, kc_ref[...].astype(BF16), vc_ref[...].astype(BF16), m0)
    y_ref[...] = (o * _silu(g_ref[...])).astype(y_ref.dtype)


def _na_bias_table(rpb, img_rows):
    kh = min(NA_WIN_H, img_rows)
    cols = np.arange(GRID_W)
    col_start = np.clip(cols - NA_WIN_W // 2, 0, GRID_W - NA_WIN_W)
    kx = np.arange(GRID_W)
    inwin = (kx[None, :] >= col_start[:, None]) & (kx[None, :] < col_start[:, None] + NA_WIN_W)
    dx = kx[None, :] - cols[:, None] + (NA_WIN_W - 1)
    onehot = (np.arange(2 * NA_WIN_W - 1)[:, None, None] == dx[None]) & inwin[None]
    toep = jnp.einsum('hyd,dxk->hyxk', rpb, jnp.asarray(onehot, rpb.dtype), precision=lax.Precision.HIGHEST)
    toep = jnp.where(inwin[None, None], toep, NEG_BIG)
    tabs = [toep[:, NA_WIN_H - 1 - dl:NA_WIN_H - 1 - dl + kh] for dl in range(kh)]
    tab = jnp.stack(tabs, axis=1).transpose(0, 1, 3, 2, 4)
    return tab.reshape(rpb.shape[0], kh, GRID_W, kh * GRID_W)


def _na(proj_b, bias_tab, geom, db):
    m = proj_b.shape[0]
    npair = db // LANES
    img_rows = geom.seq // GRID_W
    rb = _pick(geom.seq, (8 * GRID_W, 4 * GRID_W, 2 * GRID_W, GRID_W))
    n_lat_steps = geom.seq // rb
    nblk = db // LANES
    ctx_blk = geom.n_lat // geom.ctx
    kh = min(NA_WIN_H, img_rows)
    y = pl.pallas_call(
        functools.partial(_na_kernel, rows_per_step=rb // GRID_W, img_rows=img_rows),
        out_shape=jax.ShapeDtypeStruct((m, db), BF16),
        grid=(geom.batch, npair, n_lat_steps),
        in_specs=[
            pl.BlockSpec((rb, LANES), lambda b, p, s: (b * n_lat_steps + s, 1 * nblk + p)),
            pl.BlockSpec((rb, LANES), lambda b, p, s: (b * n_lat_steps + s, 4 * nblk + p)),
            pl.BlockSpec((geom.seq, LANES), lambda b, p, s: (b, 2 * nblk + p)),
            pl.BlockSpec((geom.seq, LANES), lambda b, p, s: (b, 3 * nblk + p)),
            pl.BlockSpec((geom.ctx, LANES), lambda b, p, s: (ctx_blk + b, 2 * nblk + p)),
            pl.BlockSpec((geom.ctx, LANES), lambda b, p, s: (ctx_blk + b, 3 * nblk + p)),
            pl.BlockSpec((2, kh, GRID_W, kh * GRID_W), lambda b, p, s: (p, 0, 0, 0)),
        ],
        out_specs=pl.BlockSpec((rb, LANES), lambda b, p, s: (b * n_lat_steps + s, p)),
        scratch_shapes=[pltpu.VMEM((geom.seq, LANES), BF16), pltpu.VMEM((geom.seq, LANES), BF16),
                        pltpu.VMEM((geom.ctx, LANES), BF16), pltpu.VMEM((geom.ctx, LANES), BF16)],
        compiler_params=_cparams(("parallel", "parallel", "arbitrary")),
        name="nbr_attention",
    )(proj_b, proj_b, proj_b, proj_b, proj_b, proj_b, bias_tab)
    ctx_spec = lambda sec: pl.BlockSpec((geom.ctx, LANES), lambda b, p: (ctx_blk + b, sec * nblk + p))
    return pl.pallas_call(
        _ctx_attn_kernel,
        out_shape=jax.ShapeDtypeStruct((m, db), BF16),
        grid=(geom.batch, npair),
        in_specs=[ctx_spec(1), ctx_spec(4), ctx_spec(2), ctx_spec(3), pl.BlockSpec(memory_space=pl.ANY)],
        out_specs=pl.BlockSpec((geom.ctx, LANES), lambda b, p: (ctx_blk + b, p)),
        input_output_aliases={4: 0},
        compiler_params=_cparams(("parallel", "parallel")),
        name="ctx_attention",
    )(proj_b, proj_b, proj_b, proj_b, y)


def _rglru_kernel(*refs, geom, tc, gwc, reverse):
    if reverse:
        (u_ref, wa_ref, ba_ref, wx_ref, bx_ref, lam_ref, hf_ref, g_ref,
         out_ref, carry_ref, sa_ref, sb_ref, so_ref) = refs
    else:
        (x_ref, hp_ref, hn_ref, cw_ref, cb_ref, wa_ref, ba_ref, wx_ref, bx_ref, lam_ref,
         out_ref, u_out_ref, carry_ref, sa_ref, sb_ref, so_ref) = refs
    b, c = pl.program_id(0), pl.program_id(2)

    @pl.when(c == 0)
    def _():
        carry_ref[...] = jnp.zeros(carry_ref.shape, F32)

    if reverse:
        u = u_ref[...]
    else:
        row0 = geom.chunk_block(b, c, tc, reverse) * tc
        is_first, is_last = geom.seq_edges(row0, tc)
        x = x_ref[...]
        nb = _shift_rows(x, hp_ref, hn_ref, slice(None), is_first, is_last, CONV_LEFT, CONV_W - 1 - CONV_LEFT)
        nb[0] = x
        u = cb_ref[...]
        for j in range(CONV_W):
            u = u + cw_ref[j:j + 1, :] * nb[j - CONV_LEFT]
        u_out_ref[...] = u
    gate_r = _sigmoid(_bdot(u, wa_ref[0]) + ba_ref[...])
    gate_i = _sigmoid(_bdot(u, wx_ref[0]) + bx_ref[...])
    neg_lam = -lam_ref[...]
    softplus = jnp.maximum(neg_lam, 0.0) + jnp.log1p(jnp.exp(-jnp.abs(neg_lam)))
    log_a = (-RGLRU_C) * gate_r * softplus
    a = jnp.exp(log_a)
    th = jnp.tanh(log_a)
    bv = jnp.sqrt(-2.0 * th / (1.0 - th)) * (gate_i * u)
    for j in range(gwc // LANES):
        sa_ref[j] = a[:, j * LANES:(j + 1) * LANES]
        sb_ref[j] = bv[:, j * LANES:(j + 1) * LANES]
    span = SEG * SUBLANES
    r8 = lax.broadcasted_iota(jnp.int32, (SUBLANES, 1), 0)
    first, last = (SUBLANES - 1, 0) if reverse else (0, SUBLANES - 1)
    steps = list(reversed(range(SEG))) if reverse else list(range(SEG))
    for j in range(gwc // LANES):
        carry = carry_ref[0:1, j * LANES:(j + 1) * LANES]
        for q in (reversed(range(tc // span)) if reverse else range(tc // span)):
            rows = [pl.ds(q * span + i, SUBLANES, stride=SEG) for i in range(SEG)]
            av = [sa_ref[j, rw, :] for rw in rows]
            bw = [sb_ref[j, rw, :] for rw in rows]
            hs, ps = {}, {}
            h_run, p_run = bw[steps[0]], av[steps[0]]
            hs[steps[0]], ps[steps[0]] = h_run, p_run
            for i in steps[1:]:
                h_run = av[i] * h_run + bw[i]
                p_run = av[i] * p_run
                hs[i], ps[i] = h_run, p_run
            pp, hh = p_run, h_run
            s_ = 1
            while s_ < SUBLANES:
                if reverse:
                    p_s, h_s = pltpu.roll(pp, SUBLANES - s_, 0), pltpu.roll(hh, SUBLANES - s_, 0)
                    valid = r8 < SUBLANES - s_
                else:
                    p_s, h_s = pltpu.roll(pp, s_, 0), pltpu.roll(hh, s_, 0)
                    valid = r8 >= s_
                hh = jnp.where(valid, pp * h_s + hh, hh)
                pp = jnp.where(valid, pp * p_s, pp)
                s_ *= 2
            state = hh + pp * carry
            shifted = pltpu.roll(state, (SUBLANES - 1) if reverse else 1, 0)
            cin = jnp.where(r8 == first, carry, shifted)
            carry = state[last:last + 1, :]
            for i in range(SEG):
                so_ref[j, rows[i], :] = hs[i] + ps[i] * cin
        carry_ref[:, j * LANES:(j + 1) * LANES] = jnp.broadcast_to(carry, (SUBLANES, LANES))
    h = jnp.concatenate([so_ref[j] for j in range(gwc // LANES)], axis=1)
    if reverse:
        out_ref[...] = ((hf_ref[...] + h) * _silu(g_ref[...])).astype(out_ref.dtype)
    else:
        out_ref[...] = h


def _rglru(proj_c, geom, tc, gwc, conv_w, conv_b, wa_g, ba, wx_g, bx, lam, reverse, fwd=None):
    m = proj_c.shape[0]
    dc = conv_w.shape[1]
    ngrp = dc // gwc
    nch = (geom.seq + geom.ctx) // tc
    nb8 = m // SUBLANES
    r8 = tc // SUBLANES
    blk = lambda b, g, c: geom.chunk_block(b, c, tc, reverse)
    vec = lambda rows: pl.BlockSpec((rows, gwc), lambda b, g, c: (0, g))
    tok = pl.BlockSpec((tc, gwc), lambda b, g, c: (blk(b, g, c), g))
    gate_specs = [pl.BlockSpec((1, gwc, gwc), lambda b, g, c: (g, 0, 0)), vec(1),
                  pl.BlockSpec((1, gwc, gwc), lambda b, g, c: (g, 0, 0)), vec(1), vec(1)]
    gate_args = [wa_g, ba, wx_g, bx, lam]
    if reverse:
        h_f, u = fwd
        in_specs = [tok] + gate_specs + [tok, pl.BlockSpec((tc, gwc), lambda b, g, c: (blk(b, g, c), ngrp + g))]
        args = [u] + gate_args + [h_f, proj_c]
        out_shape = jax.ShapeDtypeStruct((m, dc), BF16)
        out_specs = tok
    else:
        in_specs = [
            tok,
            pl.BlockSpec((SUBLANES, gwc), lambda b, g, c: (jnp.maximum(blk(b, g, c) * r8 - 1, 0), g)),
            pl.BlockSpec((SUBLANES, gwc), lambda b, g, c: (jnp.minimum((blk(b, g, c) + 1) * r8, nb8 - 1), g)),
            vec(CONV_W), vec(1),
        ] + gate_specs
        args = [proj_c, proj_c, proj_c, conv_w, conv_b] + gate_args
        out_shape = (jax.ShapeDtypeStruct((m, dc), F32), jax.ShapeDtypeStruct((m, dc), F32))
        out_specs = (tok, tok)
    return pl.pallas_call(
        functools.partial(_rglru_kernel, geom=geom, tc=tc, gwc=gwc, reverse=reverse),
        out_shape=out_shape,
        grid=(geom.batch, ngrp, nch),
        in_specs=in_specs,
        out_specs=out_specs,
        scratch_shapes=[pltpu.VMEM((SUBLANES, gwc), F32)] + [pltpu.VMEM((gwc // LANES, tc, LANES), F32)] * 3,
        compiler_params=_cparams(("parallel", "parallel", "arbitrary")),
        name="rglru_bwd" if reverse else "rglru_fwd",
    )(*args)


def _group_blockdiag(w, gwc):
    nblk, bs, _ = w.shape
    per = gwc // bs
    wg = w.reshape(nblk // per, per, bs, bs)
    eye = jnp.eye(per, dtype=w.dtype)
    dense = jnp.einsum('gpij,pq->gpiqj', wg, eye)
    return dense.reshape(nblk // per, gwc, gwc)


def kernel(x, c, ctx, c_ctx, mod_w, mod_b, norm_pre, norm_post, ev_w_in, ev_mu, ev_w0, ev_w_up, ev_a0, ev_a_up, ev_k_k, ev_k_a, ev_r_k, ev_gn_w, ev_gn_b, ev_rpb, ev_w_out, od_w_in, od_conv_w, od_conv_b, od_gate_a_w, od_gate_a_b, od_gate_x_w, od_gate_x_b, od_lambda, od_w_out):
    batch, seq, d = x.shape
    n_ctx = ctx.shape[1]
    depth = mod_w.shape[0]
    da = ev_k_k.shape[1]
    rw = ev_w_up.shape[2]
    db = ev_w_out.shape[1] - da
    dc = od_conv_w.shape[2]
    rg_bs = od_gate_a_w.shape[-1]
    assert da % LANES == 0 and db == da and 2 * rw == LANES and ev_a_up.shape[2] == rw
    assert batch < SUBLANES and seq % GRID_W == 0 and seq % n_ctx == 0 and n_ctx % CHUNK == 0
    assert n_ctx % GRID_W == 0 and (batch * seq) % n_ctx == 0
    geom = _Geom(batch, seq, n_ctx)
    m, n_lat = geom.m, geom.n_lat

    tm = _pick(math.gcd(n_lat, batch * n_ctx), (512, 256, 128, 64))
    tp = _pick(math.gcd(seq, n_ctx), (256, 128, 64))
    lat_blocks = n_lat // tm
    per_batch = seq // tm
    mod_row = lambda i: jnp.where(i < lat_blocks, i // per_batch, batch)

    c8 = jnp.zeros((SUBLANES, d), F32).at[:batch].set(c).at[batch].set(c_ctx)
    mods = _modulation(c8, mod_w, mod_b).reshape(depth, SUBLANES, 1, 3 * d)

    xs = [x.reshape(n_lat, d), ctx.reshape(batch * n_ctx, d)]

    heads_a = da // HEAD
    sel_np = np.zeros((da, LANES), np.float32)
    sel_np[np.arange(da), np.arange(da) // HEAD] = 1.0
    sel = jnp.asarray(sel_np, BF16)
    selt = jnp.asarray(sel_np.T, BF16)
    assert heads_a <= LANES

    a_shifted = 3 * da + 4 * rw
    gwc = rg_bs * LANES // math.gcd(rg_bs, LANES)

    for layer in range(depth):
        last = layer == depth - 1
        i = layer // 2
        mods_l = mods[layer]
        g_pre = norm_pre[layer].reshape(1, d)
        g_post = norm_post[layer].reshape(1, d)
        m_out = n_lat if last else m
        h = _normmod(xs, mods_l, g_pre, tm, mod_row)
        if layer % 2 == 0:
            w_in = ev_w_in[i].astype(BF16)
            proj_a = _inproj(h, w_in[:, :a_shifted])
            proj_b = _inproj(h, w_in[:, a_shifted:])
            pad_dir = lambda w: jnp.stack([
                jnp.concatenate([w[0], jnp.zeros_like(w[1])], axis=0),
                jnp.concatenate([jnp.zeros_like(w[0]), w[1]], axis=0)]).astype(BF16)
            r, v, kap, kd, bb, lw = _rwkv_prep(
                proj_a, geom, tp, ev_mu[i], ev_w0[i], pad_dir(ev_w_up[i]), ev_a0[i], pad_dir(ev_a_up[i]),
                ev_k_k[i].reshape(1, da), ev_k_a[i].reshape(1, da), sel, selt)
            o_f, o_b = _rwkv_scan(r, v, kap, kd, bb, lw, geom)
            y_a = _rwkv_readout(o_f, o_b, r, v, kd, proj_b, ev_r_k[i].reshape(1, da),
                                ev_gn_w[i].reshape(1, da), ev_gn_b[i].reshape(1, da), sel, selt, tp)
            y_b = _na(proj_b, _na_bias_table(ev_rpb[i], seq // GRID_W), geom, db)
            w_out = ev_w_out[i].astype(BF16)
            xs = [_outproj([y_a, y_b], [w_out[:da], w_out[da:]], xs, mods_l, g_post, tm, mod_row, m_out)]
        else:
            proj_c = _inproj(h, od_w_in[i].astype(BF16))
            row = lambda a: a.reshape(1, dc)
            out = None
            for dd in range(2):
                out = _rglru(proj_c, geom, tp, gwc, od_conv_w[i], row(od_conv_b[i]),
                             _group_blockdiag(od_gate_a_w[i, dd], gwc).astype(BF16), row(od_gate_a_b[i, dd]),
                             _group_blockdiag(od_gate_x_w[i, dd], gwc).astype(BF16), row(od_gate_x_b[i, dd]),
                             row(od_lambda[i, dd]), dd == 1, out)
            xs = [_outproj([out], [od_w_out[i].astype(BF16)], xs, mods_l, g_post, tm, mod_row, m_out)]
    return xs[0].reshape(batch, seq, d)
```

```python
import functools
import math

import jax
import jax.numpy as jnp
import numpy as np
from jax import lax
from jax.experimental import pallas as pl
from jax.experimental.pallas import tpu as pltpu

F32 = jnp.float32
BF16 = jnp.bfloat16

EPS_RMS = 1e-6
GN_EPS = 64e-5
RGLRU_C = 8.0
HEAD = 64
GRID_W = 64
NA_WIN_H = 8
NA_WIN_W = 16
CONV_W = 4
CONV_LEFT = 2
LANES = 128
SUBLANES = 8
SEG = 4
CHUNK = 64
NEG_BIG = -1e30
VMEM_LIMIT = 56 * 1024 * 1024


def _cparams(sem):
    return pltpu.CompilerParams(dimension_semantics=sem, vmem_limit_bytes=VMEM_LIMIT)


def _pick(n, cands):
    for c in cands:
        if n % c == 0:
            return c
    raise ValueError(f"no tile for {n} in {cands}")


def _bdot(a, b):
    return jnp.dot(a.astype(BF16), b.astype(BF16), preferred_element_type=F32)


def _bdot_nt(a, b):
    return lax.dot_general(a.astype(BF16), b.astype(BF16), (((1,), (1,)), ((), ())),
                           preferred_element_type=F32)


def _bdot_tn(a, b):
    return lax.dot_general(a.astype(BF16), b.astype(BF16), (((0,), (0,)), ((), ())),
                           preferred_element_type=F32)


def _split3(x):
    hi = x.astype(BF16)
    r1 = x - hi.astype(F32)
    mid = r1.astype(BF16)
    lo = (r1 - mid.astype(F32)).astype(BF16)
    return hi, mid, lo


def _dot_sel_r(x, sel):
    hi, mid, lo = _split3(x)
    d = lambda t: jnp.dot(t, sel, preferred_element_type=F32)
    return d(hi) + d(mid) + d(lo)


def _cumsum_rows(x, reverse):
    n = x.shape[0] // SUBLANES
    x3 = x.reshape(n, SUBLANES, x.shape[1])
    r8 = lax.broadcasted_iota(jnp.int32, (1, SUBLANES, 1), 1)
    s = 1
    while s < SUBLANES:
        if reverse:
            x3 = x3 + jnp.where(r8 < SUBLANES - s, pltpu.roll(x3, SUBLANES - s, 1), 0.0)
        else:
            x3 = x3 + jnp.where(r8 >= s, pltpu.roll(x3, s, 1), 0.0)
        s *= 2
    edge = 0 if reverse else SUBLANES - 1
    groups, run = [None] * n, None
    for gi in (reversed(range(n)) if reverse else range(n)):
        groups[gi] = x3[gi] if run is None else x3[gi] + run
        run = groups[gi][edge:edge + 1, :]
    return jnp.concatenate(groups, axis=0)


def _sigmoid(x):
    return 1.0 / (1.0 + jnp.exp(-x))


def _silu(x):
    return x * _sigmoid(x)


def _mod_kernel(c_ref, w_ref, b_ref, o_ref):
    cs = _silu(c_ref[...])
    o_ref[...] = _bdot(cs, w_ref[...]) + b_ref[...]


def _modulation(c8, mod_w, mod_b):
    depth, d, n = mod_w.shape
    tn = _pick(n, (1024, 512, 256, 128))
    return pl.pallas_call(
        _mod_kernel,
        out_shape=jax.ShapeDtypeStruct((depth, SUBLANES, n), F32),
        grid=(depth, n // tn),
        in_specs=[
            pl.BlockSpec((SUBLANES, d), lambda l, j: (0, 0)),
            pl.BlockSpec((None, d, tn), lambda l, j: (l, 0, j)),
            pl.BlockSpec((None, 1, tn), lambda l, j: (l, 0, j)),
        ],
        out_specs=pl.BlockSpec((None, SUBLANES, tn), lambda l, j: (l, 0, j)),
        compiler_params=_cparams(("parallel", "parallel")),
        name="modulation",
    )(c8, mod_w, mod_b.reshape(depth, 1, n))


def _stream_specs(xs, tm):
    d = xs[0].shape[1]
    if len(xs) == 1:
        return [pl.BlockSpec((tm, d), lambda i: (i, 0))]
    nl = xs[0].shape[0] // tm
    return [pl.BlockSpec((tm, d), lambda i: (jnp.minimum(i, nl - 1), 0)),
            pl.BlockSpec((tm, d), lambda i: (jnp.maximum(i - nl, 0), 0))]


def _stream_block(x_refs, n_lat_blocks):
    if len(x_refs) == 1:
        return x_refs[0][...]
    return jnp.where(pl.program_id(0) < n_lat_blocks, x_refs[0][...], x_refs[1][...])


def _normmod_kernel(*refs, d, n_lat_blocks):
    x_refs, (mod_ref, g_ref, h_ref) = refs[:-3], refs[-3:]
    x = _stream_block(x_refs, n_lat_blocks)
    ms = jnp.mean(x * x, axis=-1, keepdims=True)
    y = x * lax.rsqrt(ms + EPS_RMS) * g_ref[...]
    h_ref[...] = (y * (1.0 + mod_ref[:, d:2 * d]) + mod_ref[:, 0:d]).astype(h_ref.dtype)


def _normmod(xs, mods_l, g, tm, mod_row):
    m, d = sum(x.shape[0] for x in xs), xs[0].shape[1]
    return pl.pallas_call(
        functools.partial(_normmod_kernel, d=d, n_lat_blocks=xs[0].shape[0] // tm),
        out_shape=jax.ShapeDtypeStruct((m, d), BF16),
        grid=(m // tm,),
        in_specs=_stream_specs(xs, tm) + [
            pl.BlockSpec((None, 1, 3 * d), lambda i: (mod_row(i), 0, 0)),
            pl.BlockSpec((1, d), lambda i: (0, 0)),
        ],
        out_specs=pl.BlockSpec((tm, d), lambda i: (i, 0)),
        compiler_params=_cparams(("parallel",)),
        name="normmod",
    )(*xs, mods_l, g)


def _inproj_kernel(h_ref, w_ref, o_ref):
    o_ref[...] = jnp.dot(h_ref[...], w_ref[...], preferred_element_type=F32)


def _inproj(h, w_bf):
    m, d = h.shape
    n = w_bf.shape[1]
    tm = _pick(m, (1024, 512, 256, 128, 64))
    tn = _pick(n, (1664, 1408, 1280, 1024, 768, 640, 512, 384, 256, 128))
    return pl.pallas_call(
        _inproj_kernel,
        out_shape=jax.ShapeDtypeStruct((m, n), F32),
        grid=(n // tn, m // tm),
        in_specs=[
            pl.BlockSpec((tm, d), lambda j, i: (i, 0)),
            pl.BlockSpec((d, tn), lambda j, i: (0, j)),
        ],
        out_specs=pl.BlockSpec((tm, tn), lambda j, i: (i, j)),
        compiler_params=_cparams(("parallel", "parallel")),
        name="inproj",
    )(h, w_bf)


def _outproj_kernel(*refs, n_in, d, n_lat_blocks):
    y_refs = refs[:n_in]
    w_refs = refs[n_in:2 * n_in]
    x_refs, (mod_ref, g_ref, o_ref) = refs[2 * n_in:-3], refs[-3:]
    acc = jnp.dot(y_refs[0][...], w_refs[0][...], preferred_element_type=F32)
    for i in range(1, n_in):
        acc = acc + jnp.dot(y_refs[i][...], w_refs[i][...], preferred_element_type=F32)
    ms = jnp.mean(acc * acc, axis=-1, keepdims=True)
    z = acc * lax.rsqrt(ms + EPS_RMS) * g_ref[...]
    o_ref[...] = _stream_block(x_refs, n_lat_blocks) + mod_ref[:, 2 * d:3 * d] * z


def _outproj(ys, w, xs, mods_l, g, tm, mod_row, m_out):
    d = xs[0].shape[1]
    n_in = len(ys)
    wk = ys[0].shape[1]
    assert all(y.shape[1] == wk for y in ys) and w.shape[0] == n_in * wk
    in_specs = [pl.BlockSpec((tm, wk), lambda i: (i, 0)) for _ in ys]
    in_specs += [pl.BlockSpec((wk, d), lambda i, k=k: (k, 0)) for k in range(n_in)]
    in_specs += _stream_specs(xs, tm) + [
        pl.BlockSpec((None, 1, 3 * d), lambda i: (mod_row(i), 0, 0)),
        pl.BlockSpec((1, d), lambda i: (0, 0)),
    ]
    return pl.pallas_call(
        functools.partial(_outproj_kernel, n_in=n_in, d=d, n_lat_blocks=xs[0].shape[0] // tm),
        out_shape=jax.ShapeDtypeStruct((m_out, d), F32),
        grid=(m_out // tm,),
        in_specs=in_specs,
        out_specs=pl.BlockSpec((tm, d), lambda i: (i, 0)),
        compiler_params=_cparams(("parallel",)),
        name="outproj",
    )(*ys, *([w] * n_in), *xs, mods_l, g)


class _Geom:
    def __init__(self, batch, seq, ctx):
        self.batch, self.seq, self.ctx = batch, seq, ctx
        self.n_lat = batch * seq
        self.m = batch * (seq + ctx)

    def seq_edges(self, row0, rows):
        in_lat = row0 < self.n_lat
        off = jnp.where(in_lat, row0 % self.seq, (row0 - self.n_lat) % self.ctx)
        length = jnp.where(in_lat, self.seq, self.ctx)
        return off == 0, off + rows == length

    def chunk_block(self, b, c, rows, reverse):
        cc = self.ctx // rows
        sc = self.seq // rows
        if reverse:
            ctx_blk = self.n_lat // rows + b * cc + (cc - 1 - c)
            lat_blk = b * sc + (sc - 1 - (c - cc))
        else:
            ctx_blk = self.n_lat // rows + b * cc + c
            lat_blk = b * sc + (c - cc)
        return jnp.where(c < cc, ctx_blk, lat_blk)


def _shift_rows(x, hp_ref, hn_ref, cols, is_first, is_last, back, fwd):
    rows = x.shape[0]
    r8 = lax.broadcasted_iota(jnp.int32, (SUBLANES, 1), 0)
    hp = jnp.where(is_first, 0.0, hp_ref[:, cols])
    hn = jnp.where(is_last, 0.0, hn_ref[:, cols])
    outs = {}
    for s in range(1, back + 1):
        y = pltpu.roll(x, s, 0)
        head = jnp.where(r8 < s, pltpu.roll(hp, s, 0), y[:SUBLANES])
        outs[-s] = jnp.concatenate([head, y[SUBLANES:]], axis=0)
    for s in range(1, fwd + 1):
        y = pltpu.roll(x, rows - s, 0)
        tail = jnp.where(r8 >= SUBLANES - s, pltpu.roll(hn, SUBLANES - s, 0), y[rows - SUBLANES:])
        outs[s] = jnp.concatenate([y[:rows - SUBLANES], tail], axis=0)
    return outs


def _halo_specs(tp, width, m):
    nb8 = m // SUBLANES
    r8 = tp // SUBLANES
    return [
        pl.BlockSpec((SUBLANES, width), lambda i: (jnp.maximum(i * r8 - 1, 0), 0)),
        pl.BlockSpec((SUBLANES, width), lambda i: (jnp.minimum((i + 1) * r8, nb8 - 1), 0)),
    ]


def _rwkv_prep_kernel(f_ref, hp_ref, hn_ref, mu_ref, w0_ref, wup_ref, a0_ref, aup_ref, kk_ref, ka_ref,
                      sel_ref, selt_ref, r_ref, v_ref, kap_ref, kd_ref, bb_ref, lw_ref, *, geom, tp, da):
    i = pl.program_id(0)
    is_first, is_last = geom.seq_edges(i * tp, tp)

    def shifted(lo, hi):
        cols = slice(lo, hi)
        x = f_ref[:, cols]
        nb = _shift_rows(x, hp_ref, hn_ref, cols, is_first, is_last, 1, 1)
        return x + mu_ref[0:1, cols] * (nb[-1] - x) + mu_ref[1:2, cols] * (nb[1] - x)

    r = shifted(0, da)
    k = shifted(da, 2 * da)
    v = shifted(2 * da, 3 * da)
    cw = jnp.tanh(shifted(3 * da, 3 * da + LANES))
    ca = shifted(3 * da + LANES, 3 * da + 2 * LANES)
    r_ref[...] = r.astype(r_ref.dtype)
    v_ref[...] = v.astype(v_ref.dtype)
    kk = k * kk_ref[...]
    ss = _dot_sel_r(kk * kk, sel_ref[...])
    inv = lax.rsqrt(ss + 1e-12)
    kap = kk * _dot_sel_r(inv, selt_ref[...])
    kap_ref[...] = kap.astype(kap_ref.dtype)
    for dd in range(2):
        zw = w0_ref[dd:dd + 1, :] + _bdot(cw, wup_ref[dd])
        lw_ref[dd] = -math.exp(-0.5) * _sigmoid(zw)
        a = _sigmoid(a0_ref[dd:dd + 1, :] + _bdot(ca, aup_ref[dd]))
        kd_ref[dd] = (k * (1.0 + (a - 1.0) * ka_ref[...])).astype(kd_ref.dtype)
        bb_ref[dd] = (kap * a).astype(bb_ref.dtype)


def _rwkv_prep(proj_a, geom, tp, mu, w0, wup_pad, a0, aup_pad, k_k, k_a, sel, selt):
    m, wa = proj_a.shape
    da = k_k.shape[1]
    full = lambda a: pl.BlockSpec(a.shape, lambda i: (0,) * a.ndim)
    tok = jax.ShapeDtypeStruct((m, da), BF16)
    tok2 = jax.ShapeDtypeStruct((2, m, da), BF16)
    params = (mu, w0, wup_pad, a0, aup_pad, k_k, k_a, sel, selt)
    return pl.pallas_call(
        functools.partial(_rwkv_prep_kernel, geom=geom, tp=tp, da=da),
        out_shape=(tok, tok, tok, tok2, tok2, jax.ShapeDtypeStruct((2, m, da), F32)),
        grid=(m // tp,),
        in_specs=[pl.BlockSpec((tp, wa), lambda i: (i, 0))] + _halo_specs(tp, wa, m) + [full(p) for p in params],
        out_specs=(
            pl.BlockSpec((tp, da), lambda i: (i, 0)),
            pl.BlockSpec((tp, da), lambda i: (i, 0)),
            pl.BlockSpec((tp, da), lambda i: (i, 0)),
            pl.BlockSpec((2, tp, da), lambda i: (0, i, 0)),
            pl.BlockSpec((2, tp, da), lambda i: (0, i, 0)),
            pl.BlockSpec((2, tp, da), lambda i: (0, i, 0)),
        ),
        compiler_params=_cparams(("parallel",)),
        name="rwkv_prep",
    )(proj_a, proj_a, proj_a, *params)


def _rwkv_chunk_terms(jobs):
    L = CHUNK
    each = lambda f, *cols: [f(*a) for a in zip(*cols)]
    r, v, kap, kd, bb, lw, consts = (list(c) for c in zip(*jobs))
    reverse, strict, incl, bd_full, m0, m1, eye, last_row, levels = (list(c) for c in zip(*consts))
    stack = lambda x: jnp.concatenate([x * m0[0], x * m1[0]], axis=0).astype(BF16)

    cum = each(_cumsum_rows, lw, reverse)
    tot = each(lambda c, lr: c[lr:lr + 1, :], cum, last_row)
    rt = each(lambda x, c: (x * jnp.exp(c)).astype(BF16), r, cum)
    e_neg = each(lambda c: jnp.exp(-c), cum)
    kt = each(lambda x, e: x * e, kd, e_neg)
    bt = each(lambda x, e: x * e, bb, e_neg)
    kapt = each(lambda x, c, l: (x * jnp.exp(c - l)).astype(BF16), kap, cum, lw)
    vv_s = each(stack, v)
    g = each(lambda ka, x, b_, k_: _bdot_nt(jnp.concatenate([ka, x], axis=0),
                                            jnp.concatenate([stack(b_), stack(k_)], axis=0)),
             kapt, rt, bt, kt)
    n = each(lambda m_, x: jnp.where(m_, -x[:L, :2 * L], 0.0), strict, g)
    mk = each(lambda m_, x: jnp.where(m_, x[:L, 2 * L:], 0.0), strict, g)
    arb = each(lambda m_, x: jnp.where(m_, x[L:, :2 * L], 0.0), incl, g)
    ark = each(lambda m_, x: jnp.where(m_, x[L:, 2 * L:], 0.0), incl, g)
    mkz = each(lambda m_, a_, vv: _bdot(jnp.concatenate([m_, a_], axis=0), vv), mk, ark, vv_s)
    tm = each(lambda e, lv, x: e + jnp.where(lv[0], x, 0.0), eye, levels, n)
    for li in range(1, len(levels[0])):
        x = each(lambda lv, n_, t: _bdot(jnp.where(lv[li], n_, 0.0), stack(t)), levels, n, tm)
        tm = each(lambda t, x_: t + _bdot(t, stack(x_)), tm, x)
    wy = each(lambda t, ka, x: _bdot(t, jnp.concatenate([stack(ka), stack(x[:L])], axis=1)), tm, kapt, mkz)
    z = each(lambda x: x[L:], mkz)
    rest = each(lambda t, c: jnp.exp(t - c), tot, cum)
    gk = each(lambda v_, k_, e: _bdot_tn(v_, k_ * e), v, kd, rest)
    wrt = each(lambda x, rt_: jnp.concatenate([x[:, :LANES].astype(BF16), rt_], axis=0), wy, rt)
    y = each(lambda x: x[:, LANES:], wy)
    bp = each(lambda b_, e: (b_ * e).astype(BF16), bb, rest)
    return list(zip(wrt, y, arb, z, bp, gk, tot))


def _rwkv_chunk_states(s_prev, terms, consts):
    L = CHUNK
    each = lambda f, *cols: [f(*a) for a in zip(*cols)]
    wrt, y, arb, z, bp, gk, tot = (list(c) for c in zip(*terms))
    bd_full, m0, m1 = consts[3], consts[4], consts[5]
    stack = lambda x: jnp.concatenate([x * m0, x * m1], axis=0).astype(BF16)
    wr = each(_bdot_nt, wrt, s_prev)
    u = each(lambda x, y_: -(x[:L] + y_), wr, y)
    o = each(lambda x, a, u_, z_: x[L:] + _bdot(a, stack(u_)) + z_, wr, arb, u, z)
    s_new = each(lambda s, t, u_, b_, g_: jnp.where(bd_full, s * jnp.exp(t) + _bdot_tn(u_, b_) + g_, 0.0),
                 s_prev, tot, u, bp, gk)
    return s_new, o


def _rwkv_consts(reverse):
    L = CHUNK
    ri = lax.broadcasted_iota(jnp.int32, (L, 2 * L), 0)
    ci = lax.broadcasted_iota(jnp.int32, (L, 2 * L), 1) & (L - 1)
    strict = (ci > ri) if reverse else (ci < ri)
    incl = (ci >= ri) if reverse else (ci <= ri)
    lane = lax.broadcasted_iota(jnp.int32, (1, LANES), 1)
    m0 = jnp.where(lane < HEAD, 1.0, 0.0)
    m1 = 1.0 - m0
    eye = jnp.where(ri == ci, 1.0, 0.0)
    rs = lax.broadcasted_iota(jnp.int32, (2 * HEAD, 2 * HEAD), 0)
    cs = lax.broadcasted_iota(jnp.int32, (2 * HEAD, 2 * HEAD), 1)
    bd_full = (rs >= HEAD) == (cs >= HEAD)
    levels = tuple(((ri >> (l + 1)) == (ci >> (l + 1))) & ((ri >> l) != (ci >> l))
                   for l in range(int(math.log2(L))))
    return (reverse, strict, incl, bd_full, m0, m1, eye, 0 if reverse else L - 1, levels)


def _rwkv_scan_kernel(*refs, npairs, nsub):
    ins = refs[:12]
    of_ref, ob_ref, s_ref = refs[12:]

    @pl.when(pl.program_id(2) == 0)
    def _():
        s_ref[...] = jnp.zeros(s_ref.shape, F32)

    consts = [_rwkv_consts(False), _rwkv_consts(True)]
    chains = [(dd, p) for dd in range(2) for p in range(npairs)]
    sub = lambda dd, k: (nsub - 1 - k) if dd == 1 else k
    jobs = []
    for k in range(nsub):
        for dd, p in chains:
            r_ref, v_ref, kap_ref, kd_ref, bb_ref, lw_ref = ins[6 * dd:6 * dd + 6]
            rs = slice(sub(dd, k) * CHUNK, (sub(dd, k) + 1) * CHUNK)
            sl = slice(p * LANES, (p + 1) * LANES)
            jobs.append((r_ref[rs, sl], v_ref[rs, sl], kap_ref[rs, sl], kd_ref[rs, sl], bb_ref[rs, sl],
                         lw_ref[rs, sl], consts[dd]))
    terms = _rwkv_chunk_terms(jobs)
    states = [s_ref[dd, p] for dd, p in chains]
    for k in range(nsub):
        states, outs = _rwkv_chunk_states(states, terms[k * len(chains):(k + 1) * len(chains)], consts[0])
        for (dd, p), o in zip(chains, outs):
            o_ref = ob_ref if dd == 1 else of_ref
            o_ref[sub(dd, k) * CHUNK:(sub(dd, k) + 1) * CHUNK, p * LANES:(p + 1) * LANES] = o
    for (dd, p), s_new in zip(chains, states):
        s_ref[dd, p] = s_new


def _rwkv_scan(r, v, kap, kd, bb, lw, geom):
    m, da = r.shape
    gw = _pick(da, (1024, 512, 256, 128))
    npairs = gw // LANES
    rows = _pick(geom.ctx, (4 * CHUNK, 2 * CHUNK, CHUNK))
    nsteps = (geom.seq + geom.ctx) // rows
    in_specs = []
    for dd in range(2):
        blk = lambda b, g, c, dd=dd: geom.chunk_block(b, c, rows, dd == 1)
        tok = pl.BlockSpec((rows, gw), lambda b, g, c, blk=blk: (blk(b, g, c), g))
        tokd = pl.BlockSpec((None, rows, gw), lambda b, g, c, blk=blk, dd=dd: (dd, blk(b, g, c), g))
        in_specs += [tok, tok, tok, tokd, tokd, tokd]
    out_specs = tuple(
        pl.BlockSpec((rows, gw), lambda b, g, c, dd=dd: (geom.chunk_block(b, c, rows, dd == 1), g))
        for dd in range(2))
    tok_shape = jax.ShapeDtypeStruct((m, da), F32)
    return pl.pallas_call(
        functools.partial(_rwkv_scan_kernel, npairs=npairs, nsub=rows // CHUNK),
        out_shape=(tok_shape, tok_shape),
        grid=(geom.batch, da // gw, nsteps),
        in_specs=in_specs,
        out_specs=out_specs,
        scratch_shapes=[pltpu.VMEM((2, npairs, LANES, LANES), F32)],
        compiler_params=_cparams(("parallel", "parallel", "arbitrary")),
        name="rwkv_scan",
    )(r, v, kap, kd, bb, lw, r, v, kap, kd, bb, lw)


def _rwkv_readout_kernel(of_ref, ob_ref, r_ref, v_ref, kd_ref, g_ref, rk_ref, gw_ref, gb_ref,
                         sel_ref, selt_ref, y_ref):
    sel, selt = sel_ref[...], selt_ref[...]
    head_mean = lambda x: _dot_sel_r(_dot_sel_r(x, sel) * (1.0 / HEAD), selt)
    o = of_ref[...] + ob_ref[...]
    cen = o - head_mean(o)
    var = head_mean(cen * cen)
    on = cen * lax.rsqrt(var + GN_EPS) * gw_ref[...] + gb_ref[...]
    bonus = _dot_sel_r(_dot_sel_r(r_ref[...] * (kd_ref[0] + kd_ref[1]) * rk_ref[...], sel), selt)
    y_ref[...] = ((on + bonus * v_ref[...]) * _silu(g_ref[...])).astype(y_ref.dtype)


def _rwkv_readout(o_f, o_b, r, v, kd, proj_b, r_k, gn_w, gn_b, sel, selt, tr):
    m, da = r.shape
    full = lambda a: pl.BlockSpec(a.shape, lambda i: (0,) * a.ndim)
    tok = pl.BlockSpec((tr, da), lambda i: (i, 0))
    params = (r_k, gn_w, gn_b, sel, selt)
    return pl.pallas_call(
        _rwkv_readout_kernel,
        out_shape=jax.ShapeDtypeStruct((m, da), BF16),
        grid=(m // tr,),
        in_specs=[tok, tok, tok, tok, pl.BlockSpec((2, tr, da), lambda i: (0, i, 0)), tok] + [full(p) for p in params],
        out_specs=tok,
        compiler_params=_cparams(("parallel",)),
        name="rwkv_readout",
    )(o_f, o_b, r, v, kd, proj_b, *params)


def _attn_rows(qs, k_loc, v_loc, bias, kc, vc, m0):
    each = lambda f, *cols: [f(*a) for a in zip(*cols)]
    n = qs[0].shape[0]
    q2 = each(lambda q: (jnp.concatenate([q * m0, q * (1.0 - m0)], axis=0) * (HEAD ** -0.5)).astype(BF16), qs)
    rows2 = 2 * n
    split = lambda x: [x[i * rows2:(i + 1) * rows2] for i in range(len(qs))]
    s_ctx = split(_bdot_nt(jnp.concatenate(q2, axis=0), kc))
    mx = each(lambda s: jnp.max(s, axis=-1, keepdims=True), s_ctx)
    if k_loc is not None:
        s_loc = each(lambda q, k, b: _bdot_nt(q, k) + b, q2, k_loc, bias)
        mx = each(lambda m_, s: jnp.maximum(m_, jnp.max(s, axis=-1, keepdims=True)), mx, s_loc)
        p_loc = each(lambda s, m_: jnp.exp(s - m_), s_loc, mx)
    p_ctx = each(lambda s, m_: jnp.exp(s - m_), s_ctx, mx)
    den = each(lambda p: jnp.sum(p, axis=-1, keepdims=True), p_ctx)
    acc = split(_bdot(jnp.concatenate([p.astype(BF16) for p in p_ctx], axis=0), vc))
    if k_loc is not None:
        den = each(lambda d_, p: d_ + jnp.sum(p, axis=-1, keepdims=True), den, p_loc)
        acc = each(lambda a, p, v: a + _bdot(p, v), acc, p_loc, v_loc)
    acc = each(lambda a, d_: a / d_, acc, den)
    return each(lambda a: jnp.where(m0 > 0.5, a[:n], a[n:]), acc)


def _na_kernel(q_ref, g_ref, k_ref, v_ref, kc_ref, vc_ref, bias_ref, y_ref, kb_ref, vb_ref, kcb_ref, vcb_ref,
               *, rows_per_step, img_rows):
    step = pl.program_id(2)
    lane = lax.broadcasted_iota(jnp.int32, (1, LANES), 1)
    m0 = jnp.where(lane < HEAD, 1.0, 0.0)
    kh = min(NA_WIN_H, img_rows)

    @pl.when(step == 0)
    def _():
        kb_ref[...] = k_ref[...].astype(BF16)
        vb_ref[...] = v_ref[...].astype(BF16)
        kcb_ref[...] = kc_ref[...].astype(BF16)
        vcb_ref[...] = vc_ref[...].astype(BF16)

    qs, k_loc, v_loc, bias = [], [], [], []
    for yy in range(rows_per_step):
        y = step * rows_per_step + yy
        y0 = jnp.clip(y - kh // 2, 0, img_rows - kh)
        start = pl.multiple_of(y0 * GRID_W, GRID_W)
        k_loc.append(kb_ref[pl.ds(start, kh * GRID_W), :])
        v_loc.append(vb_ref[pl.ds(start, kh * GRID_W), :])
        bias.append(bias_ref[:, y - y0].reshape(2 * GRID_W, kh * GRID_W))
        qs.append(q_ref[yy * GRID_W:(yy + 1) * GRID_W, :])
    outs = _attn_rows(qs, k_loc, v_loc, bias, kcb_ref[...], vcb_ref[...], m0)
    for yy, o in enumerate(outs):
        rs = slice(yy * GRID_W, (yy + 1) * GRID_W)
        y_ref[rs, :] = (o * _silu(g_ref[rs, :])).astype(y_ref.dtype)


def _ctx_attn_kernel(q_ref, g_ref, kc_ref, vc_ref, y_in_ref, y_ref):
    del y_in_ref
    lane = lax.broadcasted_iota(jnp.int32, (1, LANES), 1)
    m0 = jnp.where(lane < HEAD, 1.0, 0.0)
    o, = _attn_rows([q_ref[...]], None, None, None, kc_ref[...].astype(BF16), vc_ref[...].astype(BF16), m0)
    y_ref[...] = (o * _silu(g_ref[...])).astype(y_ref.dtype)


def _na_bias_table(rpb, img_rows):
    kh = min(NA_WIN_H, img_rows)
    cols = np.arange(GRID_W)
    col_start = np.clip(cols - NA_WIN_W // 2, 0, GRID_W - NA_WIN_W)
    kx = np.arange(GRID_W)
    inwin = (kx[None, :] >= col_start[:, None]) & (kx[None, :] < col_start[:, None] + NA_WIN_W)
    dx = kx[None, :] - cols[:, None] + (NA_WIN_W - 1)
    onehot = (np.arange(2 * NA_WIN_W - 1)[:, None, None] == dx[None]) & inwin[None]
    toep = jnp.einsum('hyd,dxk->hyxk', rpb, jnp.asarray(onehot, rpb.dtype), precision=lax.Precision.HIGHEST)
    toep = jnp.where(inwin[None, None], toep, NEG_BIG)
    tabs = [toep[:, NA_WIN_H - 1 - dl:NA_WIN_H - 1 - dl + kh] for dl in range(kh)]
    tab = jnp.stack(tabs, axis=1).transpose(0, 1, 3, 2, 4)
    return tab.reshape(rpb.shape[0], kh, GRID_W, kh * GRID_W)


def _na(proj_b, bias_tab, geom, db):
    m = proj_b.shape[0]
    npair = db // LANES
    img_rows = geom.seq // GRID_W
    rb = _pick(geom.seq, (8 * GRID_W, 4 * GRID_W, 2 * GRID_W, GRID_W))
    n_lat_steps = geom.seq // rb
    nblk = db // LANES
    ctx_blk = geom.n_lat // geom.ctx
    kh = min(NA_WIN_H, img_rows)
    y = pl.pallas_call(
        functools.partial(_na_kernel, rows_per_step=rb // GRID_W, img_rows=img_rows),
        out_shape=jax.ShapeDtypeStruct((m, db), BF16),
        grid=(geom.batch, npair, n_lat_steps),
        in_specs=[
            pl.BlockSpec((rb, LANES), lambda b, p, s: (b * n_lat_steps + s, 1 * nblk + p)),
            pl.BlockSpec((rb, LANES), lambda b, p, s: (b * n_lat_steps + s, 4 * nblk + p)),
            pl.BlockSpec((geom.seq, LANES), lambda b, p, s: (b, 2 * nblk + p)),
            pl.BlockSpec((geom.seq, LANES), lambda b, p, s: (b, 3 * nblk + p)),
            pl.BlockSpec((geom.ctx, LANES), lambda b, p, s: (ctx_blk + b, 2 * nblk + p)),
            pl.BlockSpec((geom.ctx, LANES), lambda b, p, s: (ctx_blk + b, 3 * nblk + p)),
            pl.BlockSpec((2, kh, GRID_W, kh * GRID_W), lambda b, p, s: (p, 0, 0, 0)),
        ],
        out_specs=pl.BlockSpec((rb, LANES), lambda b, p, s: (b * n_lat_steps + s, p)),
        scratch_shapes=[pltpu.VMEM((geom.seq, LANES), BF16), pltpu.VMEM((geom.seq, LANES), BF16),
                        pltpu.VMEM((geom.ctx, LANES), BF16), pltpu.VMEM((geom.ctx, LANES), BF16)],
        compiler_params=_cparams(("parallel", "parallel", "arbitrary")),
        name="nbr_attention",
    )(proj_b, proj_b, proj_b, proj_b, proj_b, proj_b, bias_tab)
    ctx_spec = lambda sec: pl.BlockSpec((geom.ctx, LANES), lambda b, p: (ctx_blk + b, sec * nblk + p))
    return pl.pallas_call(
        _ctx_attn_kernel,
        out_shape=jax.ShapeDtypeStruct((m, db), BF16),
        grid=(geom.batch, npair),
        in_specs=[ctx_spec(1), ctx_spec(4), ctx_spec(2), ctx_spec(3), pl.BlockSpec(memory_space=pl.ANY)],
        out_specs=pl.BlockSpec((geom.ctx, LANES), lambda b, p: (ctx_blk + b, p)),
        input_output_aliases={4: 0},
        compiler_params=_cparams(("parallel", "parallel")),
        name="ctx_attention",
    )(proj_b, proj_b, proj_b, proj_b, y)


def _rglru_kernel(*refs, geom, tc, gwc, reverse):
    if reverse:
        (u_ref, wa_ref, ba_ref, wx_ref, bx_ref, lam_ref, hf_ref, g_ref,
         out_ref, carry_ref, sa_ref, sb_ref, so_ref) = refs
    else:
        (x_ref, hp_ref, hn_ref, cw_ref, cb_ref, wa_ref, ba_ref, wx_ref, bx_ref, lam_ref,
         out_ref, u_out_ref, carry_ref, sa_ref, sb_ref, so_ref) = refs
    b, c = pl.program_id(0), pl.program_id(2)

    @pl.when(c == 0)
    def _():
        carry_ref[...] = jnp.zeros(carry_ref.shape, F32)

    if reverse:
        u = u_ref[...]
    else:
        row0 = geom.chunk_block(b, c, tc, reverse) * tc
        is_first, is_last = geom.seq_edges(row0, tc)
        x = x_ref[...]
        nb = _shift_rows(x, hp_ref, hn_ref, slice(None), is_first, is_last, CONV_LEFT, CONV_W - 1 - CONV_LEFT)
        nb[0] = x
        u = cb_ref[...]
        for j in range(CONV_W):
            u = u + cw_ref[j:j + 1, :] * nb[j - CONV_LEFT]
        u_out_ref[...] = u
    gate_r = _sigmoid(_bdot(u, wa_ref[0]) + ba_ref[...])
    gate_i = _sigmoid(_bdot(u, wx_ref[0]) + bx_ref[...])
    neg_lam = -lam_ref[...]
    softplus = jnp.maximum(neg_lam, 0.0) + jnp.log1p(jnp.exp(-jnp.abs(neg_lam)))
    log_a = (-RGLRU_C) * gate_r * softplus
    a = jnp.exp(log_a)
    th = jnp.tanh(log_a)
    bv = jnp.sqrt(-2.0 * th / (1.0 - th)) * (gate_i * u)
    for j in range(gwc // LANES):
        sa_ref[j] = a[:, j * LANES:(j + 1) * LANES]
        sb_ref[j] = bv[:, j * LANES:(j + 1) * LANES]
    span = SEG * SUBLANES
    r8 = lax.broadcasted_iota(jnp.int32, (SUBLANES, 1), 0)
    first, last = (SUBLANES - 1, 0) if reverse else (0, SUBLANES - 1)
    steps = list(reversed(range(SEG))) if reverse else list(range(SEG))
    for j in range(gwc // LANES):
        carry = carry_ref[0:1, j * LANES:(j + 1) * LANES]
        for q in (reversed(range(tc // span)) if reverse else range(tc // span)):
            rows = [pl.ds(q * span + i, SUBLANES, stride=SEG) for i in range(SEG)]
            av = [sa_ref[j, rw, :] for rw in rows]
            bw = [sb_ref[j, rw, :] for rw in rows]
            hs, ps = {}, {}
            h_run, p_run = bw[steps[0]], av[steps[0]]
            hs[steps[0]], ps[steps[0]] = h_run, p_run
            for i in steps[1:]:
                h_run = av[i] * h_run + bw[i]
                p_run = av[i] * p_run
                hs[i], ps[i] = h_run, p_run
            pp, hh = p_run, h_run
            s_ = 1
            while s_ < SUBLANES:
                if reverse:
                    p_s, h_s = pltpu.roll(pp, SUBLANES - s_, 0), pltpu.roll(hh, SUBLANES - s_, 0)
                    valid = r8 < SUBLANES - s_
                else:
                    p_s, h_s = pltpu.roll(pp, s_, 0), pltpu.roll(hh, s_, 0)
                    valid = r8 >= s_
                hh = jnp.where(valid, pp * h_s + hh, hh)
                pp = jnp.where(valid, pp * p_s, pp)
                s_ *= 2
            state = hh + pp * carry
            shifted = pltpu.roll(state, (SUBLANES - 1) if reverse else 1, 0)
            cin = jnp.where(r8 == first, carry, shifted)
            carry = state[last:last + 1, :]
            for i in range(SEG):
                so_ref[j, rows[i], :] = hs[i] + ps[i] * cin
        carry_ref[:, j * LANES:(j + 1) * LANES] = jnp.broadcast_to(carry, (SUBLANES, LANES))
    h = jnp.concatenate([so_ref[j] for j in range(gwc // LANES)], axis=1)
    if reverse:
        out_ref[...] = ((hf_ref[...] + h) * _silu(g_ref[...])).astype(out_ref.dtype)
    else:
        out_ref[...] = h


def _rglru(proj_c, geom, tc, gwc, conv_w, conv_b, wa_g, ba, wx_g, bx, lam, reverse, fwd=None):
    m = proj_c.shape[0]
    dc = conv_w.shape[1]
    ngrp = dc // gwc
    nch = (geom.seq + geom.ctx) // tc
    nb8 = m // SUBLANES
    r8 = tc // SUBLANES
    blk = lambda b, g, c: geom.chunk_block(b, c, tc, reverse)
    vec = lambda rows: pl.BlockSpec((rows, gwc), lambda b, g, c: (0, g))
    tok = pl.BlockSpec((tc, gwc), lambda b, g, c: (blk(b, g, c), g))
    gate_specs = [pl.BlockSpec((1, gwc, gwc), lambda b, g, c: (g, 0, 0)), vec(1),
                  pl.BlockSpec((1, gwc, gwc), lambda b, g, c: (g, 0, 0)), vec(1), vec(1)]
    gate_args = [wa_g, ba, wx_g, bx, lam]
    if reverse:
        h_f, u = fwd
        in_specs = [tok] + gate_specs + [tok, pl.BlockSpec((tc, gwc), lambda b, g, c: (blk(b, g, c), ngrp + g))]
        args = [u] + gate_args + [h_f, proj_c]
        out_shape = jax.ShapeDtypeStruct((m, dc), BF16)
        out_specs = tok
    else:
        in_specs = [
            tok,
            pl.BlockSpec((SUBLANES, gwc), lambda b, g, c: (jnp.maximum(blk(b, g, c) * r8 - 1, 0), g)),
            pl.BlockSpec((SUBLANES, gwc), lambda b, g, c: (jnp.minimum((blk(b, g, c) + 1) * r8, nb8 - 1), g)),
            vec(CONV_W), vec(1),
        ] + gate_specs
        args = [proj_c, proj_c, proj_c, conv_w, conv_b] + gate_args
        out_shape = (jax.ShapeDtypeStruct((m, dc), F32), jax.ShapeDtypeStruct((m, dc), F32))
        out_specs = (tok, tok)
    return pl.pallas_call(
        functools.partial(_rglru_kernel, geom=geom, tc=tc, gwc=gwc, reverse=reverse),
        out_shape=out_shape,
        grid=(geom.batch, ngrp, nch),
        in_specs=in_specs,
        out_specs=out_specs,
        scratch_shapes=[pltpu.VMEM((SUBLANES, gwc), F32)] + [pltpu.VMEM((gwc // LANES, tc, LANES), F32)] * 3,
        compiler_params=_cparams(("parallel", "parallel", "arbitrary")),
        name="rglru_bwd" if reverse else "rglru_fwd",
    )(*args)


def _group_blockdiag(w, gwc):
    nblk, bs, _ = w.shape
    per = gwc // bs
    wg = w.reshape(nblk // per, per, bs, bs)
    eye = jnp.eye(per, dtype=w.dtype)
    dense = jnp.einsum('gpij,pq->gpiqj', wg, eye)
    return dense.reshape(nblk // per, gwc, gwc)


def kernel(x, c, ctx, c_ctx, mod_w, mod_b, norm_pre, norm_post, ev_w_in, ev_mu, ev_w0, ev_w_up, ev_a0, ev_a_up, ev_k_k, ev_k_a, ev_r_k, ev_gn_w, ev_gn_b, ev_rpb, ev_w_out, od_w_in, od_conv_w, od_conv_b, od_gate_a_w, od_gate_a_b, od_gate_x_w, od_gate_x_b, od_lambda, od_w_out):
    batch, seq, d = x.shape
    n_ctx = ctx.shape[1]
    depth = mod_w.shape[0]
    da = ev_k_k.shape[1]
    rw = ev_w_up.shape[2]
    db = ev_w_out.shape[1] - da
    dc = od_conv_w.shape[2]
    rg_bs = od_gate_a_w.shape[-1]
    assert da % LANES == 0 and db == da and 2 * rw == LANES and ev_a_up.shape[2] == rw
    assert batch < SUBLANES and seq % GRID_W == 0 and seq % n_ctx == 0 and n_ctx % CHUNK == 0
    assert n_ctx % GRID_W == 0 and (batch * seq) % n_ctx == 0
    geom = _Geom(batch, seq, n_ctx)
    m, n_lat = geom.m, geom.n_lat

    tm = _pick(math.gcd(n_lat, batch * n_ctx), (512, 256, 128, 64))
    tp = _pick(math.gcd(seq, n_ctx), (256, 128, 64))
    lat_blocks = n_lat // tm
    per_batch = seq // tm
    mod_row = lambda i: jnp.where(i < lat_blocks, i // per_batch, batch)

    c8 = jnp.zeros((SUBLANES, d), F32).at[:batch].set(c).at[batch].set(c_ctx)
    mods = _modulation(c8, mod_w, mod_b).reshape(depth, SUBLANES, 1, 3 * d)

    xs = [x.reshape(n_lat, d), ctx.reshape(batch * n_ctx, d)]

    heads_a = da // HEAD
    sel_np = np.zeros((da, LANES), np.float32)
    sel_np[np.arange(da), np.arange(da) // HEAD] = 1.0
    sel = jnp.asarray(sel_np, BF16)
    selt = jnp.asarray(sel_np.T, BF16)
    assert heads_a <= LANES

    a_shifted = 3 * da + 4 * rw
    gwc = rg_bs * LANES // math.gcd(rg_bs, LANES)

    for layer in range(depth):
        last = layer == depth - 1
        i = layer // 2
        mods_l = mods[layer]
        g_pre = norm_pre[layer].reshape(1, d)
        g_post = norm_post[layer].reshape(1, d)
        m_out = n_lat if last else m
        h = _normmod(xs, mods_l, g_pre, tm, mod_row)
        if layer % 2 == 0:
            proj_a = _inproj(h, ev_w_in[i][:, :a_shifted].astype(BF16))
            proj_b = _inproj(h, ev_w_in[i][:, a_shifted:].astype(BF16))
            pad_dir = lambda w: jnp.stack([
                jnp.concatenate([w[0], jnp.zeros_like(w[1])], axis=0),
                jnp.concatenate([jnp.zeros_like(w[0]), w[1]], axis=0)]).astype(BF16)
            r, v, kap, kd, bb, lw = _rwkv_prep(
                proj_a, geom, tp, ev_mu[i], ev_w0[i], pad_dir(ev_w_up[i]), ev_a0[i], pad_dir(ev_a_up[i]),
                ev_k_k[i].reshape(1, da), ev_k_a[i].reshape(1, da), sel, selt)
            o_f, o_b = _rwkv_scan(r, v, kap, kd, bb, lw, geom)
            y_a = _rwkv_readout(o_f, o_b, r, v, kd, proj_b, ev_r_k[i].reshape(1, da),
                                ev_gn_w[i].reshape(1, da), ev_gn_b[i].reshape(1, da), sel, selt, tp)
            y_b = _na(proj_b, _na_bias_table(ev_rpb[i], seq // GRID_W), geom, db)
            xs = [_outproj([y_a, y_b], ev_w_out[i].astype(BF16), xs, mods_l, g_post, tm, mod_row, m_out)]
        else:
            proj_c = _inproj(h, od_w_in[i].astype(BF16))
            row = lambda a: a.reshape(1, dc)
            out = None
            for dd in range(2):
                out = _rglru(proj_c, geom, tp, gwc, od_conv_w[i], row(od_conv_b[i]),
                             _group_blockdiag(od_gate_a_w[i, dd], gwc).astype(BF16), row(od_gate_a_b[i, dd]),
                             _group_blockdiag(od_gate_x_w[i, dd], gwc).astype(BF16), row(od_gate_x_b[i, dd]),
                             row(od_lambda[i, dd]), dd == 1, out)
            xs = [_outproj([out], od_w_out[i].astype(BF16), xs, mods_l, g_post, tm, mod_row, m_out)]
    return xs[0].reshape(batch, seq, d)
```
